```python
import math
import jax
import jax.numpy as jnp
from jax import lax
import numpy as np

D_MODEL = 1024
BATCH = 1
SEQ = 16384
DEPTH = 2

S5_WIDTH = D_MODEL // 4
S5_GROUP = 16
S5_NGROUPS = S5_WIDTH // S5_GROUP
S5_STATE = 64
S5_DT_MIN = 0.001
S5_DT_MAX = 0.1
NSA_HEADS = 8
NSA_KV_GROUPS = 2
NSA_HPG = NSA_HEADS // NSA_KV_GROUPS
HEAD_DIM = 64
NSA_WIDTH = NSA_HEADS * HEAD_DIM
NSA_KV_WIDTH = NSA_KV_GROUPS * HEAD_DIM
ROPE_DIM = HEAD_DIM // 4
ROPE_THETA = 500000.0
CMP_BLOCK = 32
CMP_STRIDE = 16
CMP_HIDDEN = 256
SLC_BLOCK = 64
SLC_TOPK = 16
WINDOW = 512
Q_BLOCK = 128
FORCE_BONUS = 1.0e4
GLA_HEADS = 4
GLA_DK = 32
GLA_DV = 64
GLA_KWIDTH = GLA_HEADS * GLA_DK
GLA_WIDTH = GLA_HEADS * GLA_DV
GLA_RANK = 16
GLA_TAU = 16.0
GLA_CHUNK = 64
D_FF = 2816
CONV_WIDTH = 3
IN_SPLIT = (S5_WIDTH, NSA_WIDTH, NSA_KV_WIDTH, NSA_KV_WIDTH, NSA_KV_WIDTH, NSA_KV_WIDTH, NSA_KV_WIDTH, NSA_KV_WIDTH, 3 * NSA_HEADS, GLA_KWIDTH, GLA_KWIDTH, GLA_WIDTH, GLA_RANK, GLA_WIDTH, 3 * D_MODEL)
D_IN = sum(IN_SPLIT)
EPS = 1e-6
NEG_INF = -1e30

kernel_name = 'hybrid_s5_nsa_gla_convffn'


def rms_norm(x, g):
    xf = x.astype(jnp.float32)
    y = xf * lax.rsqrt(jnp.mean(xf * xf, axis=-1, keepdims=True) + EPS)
    return (y * g.astype(jnp.float32)).astype(x.dtype)


def rotary_tables(positions):
    inv_freq = ROPE_THETA ** (-jnp.arange(0, ROPE_DIM, 2, dtype=jnp.float32) / ROPE_DIM)
    ang = positions.astype(jnp.float32)[..., None] * inv_freq
    return jnp.cos(ang)[:, :, None, :], jnp.sin(ang)[:, :, None, :]


def apply_rotary(x, cos, sin):
    half = ROPE_DIM // 2
    c = cos.astype(x.dtype)
    s = sin.astype(x.dtype)
    x1 = x[..., :half]
    x2 = x[..., half:ROPE_DIM]
    return jnp.concatenate([x1 * c - x2 * s, x2 * c + x1 * s, x[..., ROPE_DIM:]], axis=-1)


def masked_softmax(s, mask):
    s = jnp.where(mask, s.astype(jnp.float32), NEG_INF)
    m = jnp.max(s, axis=-1, keepdims=True)
    p = jnp.where(mask, jnp.exp(s - m), 0.0)
    return p / jnp.maximum(jnp.sum(p, axis=-1, keepdims=True), 1e-30)


def s5_mixer(u, lam_re, lam_im, log_dt, b_re, b_im, c_re, c_im, d_skip, w_glu, b_glu):
    bsz, seq, _ = u.shape
    f32 = jnp.float32
    ug = u.astype(f32).reshape(bsz, seq, S5_NGROUPS, S5_GROUP)
    dt = jnp.exp(log_dt.astype(f32))[:, None]
    lr = lam_re.astype(f32)
    li = lam_im.astype(f32)
    mag = jnp.exp(lr * dt)
    ab_re = mag * jnp.cos(li * dt)
    ab_im = mag * jnp.sin(li * dt)
    nr = ab_re - 1.0
    ni = ab_im
    den = lr * lr + li * li
    f_re = (nr * lr + ni * li) / den
    f_im = (ni * lr - nr * li) / den
    br = b_re.astype(f32)
    bi = b_im.astype(f32)
    bb_re = f_re[..., None] * br - f_im[..., None] * bi
    bb_im = f_re[..., None] * bi + f_im[..., None] * br
    bu_re = jnp.einsum('bsgh,gph->bsgp', ug, bb_re)
    bu_im = jnp.einsum('bsgh,gph->bsgp', ug, bb_im)
    a_re = jnp.broadcast_to(ab_re, bu_re.shape)
    a_im = jnp.broadcast_to(ab_im, bu_re.shape)

    def combine(e1, e2):
        a1r, a1i, b1r, b1i = e1
        a2r, a2i, b2r, b2i = e2
        return (a2r * a1r - a2i * a1i,
                a2r * a1i + a2i * a1r,
                a2r * b1r - a2i * b1i + b2r,
                a2r * b1i + a2i * b1r + b2i)

    _, _, xr, xi = lax.associative_scan(combine, (a_re, a_im, bu_re, bu_im), axis=1)
    y = jnp.einsum('bsgp,ghp->bsgh', xr, c_re.astype(f32)) - jnp.einsum('bsgp,ghp->bsgh', xi, c_im.astype(f32))
    y = y.reshape(bsz, seq, S5_WIDTH) + d_skip.astype(f32) * u.astype(f32)
    y = jax.nn.gelu(y)
    y = y * jax.nn.sigmoid(y @ w_glu.astype(f32) + b_glu.astype(f32))
    return y.astype(u.dtype)


def compress_blocks(k, pe, w1, w2):
    bsz, seq, g, dh = k.shape
    n_cmp = (seq - CMP_BLOCK) // CMP_STRIDE + 1
    idx = jnp.arange(n_cmp)[:, None] * CMP_STRIDE + jnp.arange(CMP_BLOCK)[None, :]
    blk = k[:, idx] + pe[:, None, :]
    blk = jnp.moveaxis(blk, 3, 2).reshape(bsz, n_cmp, g, CMP_BLOCK * dh)
    return jax.nn.gelu(blk @ w1) @ w2


def nsa_mixer(q, k_cmp, v_cmp, k_slc, v_slc, k_win, v_win, g_branch, cos, sin, pe_k, pe_v, ck_w1, ck_w2, cv_w1, cv_w2):
    bsz, seq, _ = q.shape
    G, H, dh = NSA_KV_GROUPS, NSA_HPG, HEAD_DIM
    q = apply_rotary(q.reshape(bsz, seq, NSA_HEADS, dh), cos, sin) * (dh ** -0.5)

    def kvh(a):
        return a.reshape(bsz, seq, G, dh)

    k_cmp = apply_rotary(kvh(k_cmp), cos, sin)
    k_slc = apply_rotary(kvh(k_slc), cos, sin)
    k_win = apply_rotary(kvh(k_win), cos, sin)
    v_cmp, v_slc, v_win = kvh(v_cmp), kvh(v_slc), kvh(v_win)

    kc = compress_blocks(k_cmp, pe_k, ck_w1, ck_w2)
    vc = compress_blocks(v_cmp, pe_v, cv_w1, cv_w2)
    n_cmp = kc.shape[1]
    cmp_end = jnp.arange(n_cmp) * CMP_STRIDE + (CMP_BLOCK - 1)

    n_slc = seq // SLC_BLOCK
    topk = min(SLC_TOPK, n_slc)
    ratio = SLC_BLOCK // CMP_STRIDE
    lead = CMP_BLOCK // CMP_STRIDE - 1
    span = ratio + lead
    pad_right = ratio * n_slc - n_cmp
    k_sb = k_slc.reshape(bsz, n_slc, SLC_BLOCK, G, dh).transpose(0, 3, 1, 2, 4)
    v_sb = v_slc.reshape(bsz, n_slc, SLC_BLOCK, G, dh).transpose(0, 3, 1, 2, 4)
    b_ix = jnp.arange(bsz)[:, None, None, None]
    g_ix = jnp.arange(G)[None, :, None, None]
    blk_ids = jnp.arange(n_slc)

    k_wp = jnp.pad(k_win, ((0, 0), (WINDOW, 0), (0, 0), (0, 0)))
    v_wp = jnp.pad(v_win, ((0, 0), (WINDOW, 0), (0, 0), (0, 0)))

    n_q = seq // Q_BLOCK
    q_chunks = q.reshape(bsz, n_q, Q_BLOCK, G, H, dh).swapaxes(0, 1)
    starts = jnp.arange(n_q, dtype=jnp.int32) * Q_BLOCK

    def per_query_block(args):
        qc, start = args
        t = start + jnp.arange(Q_BLOCK)
        s = jnp.einsum('bqghd,bigd->bghqi', qc, kc)
        p_cmp = masked_softmax(s, cmp_end[None, :] <= t[:, None])
        o_cmp = jnp.einsum('bghqi,bigd->bqghd', p_cmp.astype(vc.dtype), vc)
        imp = jnp.pad(p_cmp.sum(axis=2), ((0, 0), (0, 0), (0, 0), (lead, pad_right)))
        imp_slc = imp[..., 0:ratio * (n_slc - 1) + 1:ratio]
        for sft in range(1, span):
            imp_slc = imp_slc + imp[..., sft:sft + ratio * (n_slc - 1) + 1:ratio]
        cur = t // SLC_BLOCK
        valid = (blk_ids[None, :] * SLC_BLOCK) <= t[:, None]
        forced = (blk_ids[None, :] == 0) | (blk_ids[None, :] == cur[:, None]) | (blk_ids[None, :] == cur[:, None] - 1)
        score = jnp.where(valid, imp_slc + forced.astype(jnp.float32) * FORCE_BONUS, NEG_INF)
        _, sel = lax.top_k(score, topk)
        ks = k_sb[b_ix, g_ix, sel].reshape(bsz, G, Q_BLOCK, topk * SLC_BLOCK, dh)
        vs = v_sb[b_ix, g_ix, sel].reshape(bsz, G, Q_BLOCK, topk * SLC_BLOCK, dh)
        kpos = (sel[..., None] * SLC_BLOCK + jnp.arange(SLC_BLOCK)).reshape(bsz, G, Q_BLOCK, topk * SLC_BLOCK)
        s = jnp.einsum('bqghd,bgqkd->bghqk', qc, ks)
        p = masked_softmax(s, kpos[:, :, None] <= t[:, None])
        o_slc = jnp.einsum('bghqk,bgqkd->bqghd', p.astype(vs.dtype), vs)
        kw = lax.dynamic_slice_in_dim(k_wp, start, Q_BLOCK + WINDOW, axis=1)
        vw = lax.dynamic_slice_in_dim(v_wp, start, Q_BLOCK + WINDOW, axis=1)
        kw_pos = start - WINDOW + jnp.arange(Q_BLOCK + WINDOW)
        dpos = t[:, None] - kw_pos[None, :]
        wmask = (dpos >= 0) & (dpos < WINDOW) & (kw_pos[None, :] >= 0)
        s = jnp.einsum('bqghd,bkgd->bghqk', qc, kw)
        p = masked_softmax(s, wmask)
        o_win = jnp.einsum('bghqk,bkgd->bqghd', p.astype(vw.dtype), vw)
        return o_cmp, o_slc, o_win

    o_cmp, o_slc, o_win = lax.map(per_query_block, (q_chunks, starts))

    def unchunk(o):
        return o.swapaxes(0, 1).reshape(bsz, seq, NSA_HEADS, dh)

    g = jax.nn.sigmoid(g_branch.astype(jnp.float32)).reshape(bsz, seq, NSA_HEADS, 3)
    out = g[..., 0:1] * unchunk(o_cmp) + g[..., 1:2] * unchunk(o_slc) + g[..., 2:3] * unchunk(o_win)
    return out.reshape(bsz, seq, NSA_WIDTH).astype(q.dtype)


def gla_mixer(q, k, v, a_low, r, w_a2, b_a, norm_g):
    bsz, seq, _ = q.shape
    f32 = jnp.float32
    nc = seq // GLA_CHUNK
    C, H, dk, dv = GLA_CHUNK, GLA_HEADS, GLA_DK, GLA_DV
    qf = q.astype(f32).reshape(bsz, nc, C, H, dk) * (dk ** -0.5)
    kf = k.astype(f32).reshape(bsz, nc, C, H, dk)
    vf = v.astype(f32).reshape(bsz, nc, C, H, dv)
    log_a = jax.nn.log_sigmoid((a_low @ w_a2 + b_a).astype(f32)) / GLA_TAU
    log_a = log_a.reshape(bsz, nc, C, H, dk)
    bcum = jnp.cumsum(log_a, axis=2)
    b_last = bcum[:, :, -1:]
    q_t = qf * jnp.exp(bcum)
    k_t = kf * jnp.exp(-bcum)
    k_d = kf * jnp.exp(b_last - bcum)
    causal = jnp.tril(jnp.ones((C, C), dtype=bool))
    attn = jnp.where(causal, jnp.einsum('bnihd,bnjhd->bnhij', q_t, k_t), 0.0)
    o_intra = jnp.einsum('bnhij,bnjhv->bnihv', attn, vf)
    kv = jnp.einsum('bnjhd,bnjhv->bnhdv', k_d, vf)
    decay = jnp.exp(b_last[:, :, 0])

    def step(state, inp):
        dec, kv_n = inp
        return dec[..., None] * state + kv_n, state

    _, s_prev = lax.scan(step, jnp.zeros((bsz, H, dk, dv), f32), (decay.swapaxes(0, 1), kv.swapaxes(0, 1)))
    s_prev = s_prev.swapaxes(0, 1)
    o_inter = jnp.einsum('bnihd,bnhdv->bnihv', q_t, s_prev)
    o = (o_intra + o_inter).reshape(bsz, seq, H, dv)
    o = o * lax.rsqrt(jnp.mean(o * o, axis=-1, keepdims=True) + EPS)
    o = o.reshape(bsz, seq, GLA_WIDTH) * norm_g.astype(f32) * jax.nn.silu(r.astype(f32))
    return o.astype(q.dtype)


def conv_ffn(x, w_up, conv_w, conv_b, w_down):
    seq = x.shape[1]
    h = x @ w_up
    hp = jnp.pad(h, ((0, 0), (CONV_WIDTH - 1, 0), (0, 0)))
    hc = conv_b
    for j in range(CONV_WIDTH):
        hc = hc + conv_w[j] * hp[:, j:j + seq]
    gate, val = jnp.split(hc, 2, axis=-1)
    return (jax.nn.gelu(gate) * val) @ w_down


def setup_inputs(seed: int = 0) -> dict:
    key = jax.random.key(seed)
    keys = iter(jax.random.split(key, 48))
    L = DEPTH

    def nrm(shape, scale):
        return scale * jax.random.normal(next(keys), shape, jnp.float32)

    def gain(shape):
        return 1.0 + nrm(shape, 0.02)

    inp = {}
    inp['x'] = nrm((BATCH, SEQ, D_MODEL), 1.0)
    inp['positions'] = jnp.broadcast_to(jnp.arange(SEQ, dtype=jnp.int32), (BATCH, SEQ))
    inp['norm1_g'] = gain((L, D_MODEL))
    inp['w_in'] = nrm((L, D_MODEL, D_IN), D_MODEL ** -0.5)
    inp['s5_lam_re'] = -0.5 + nrm((L, S5_NGROUPS, S5_STATE), 0.01)
    inp['s5_lam_im'] = jnp.broadcast_to(math.pi * jnp.arange(S5_STATE, dtype=jnp.float32), (L, S5_NGROUPS, S5_STATE))
    inp['s5_log_dt'] = jax.random.uniform(next(keys), (L, S5_NGROUPS), jnp.float32, math.log(S5_DT_MIN), math.log(S5_DT_MAX))
    inp['s5_b_re'] = nrm((L, S5_NGROUPS, S5_STATE, S5_GROUP), (2 * S5_GROUP) ** -0.5)
    inp['s5_b_im'] = nrm((L, S5_NGROUPS, S5_STATE, S5_GROUP), (2 * S5_GROUP) ** -0.5)
    inp['s5_c_re'] = nrm((L, S5_NGROUPS, S5_GROUP, S5_STATE), S5_STATE ** -0.5)
    inp['s5_c_im'] = nrm((L, S5_NGROUPS, S5_GROUP, S5_STATE), S5_STATE ** -0.5)
    inp['s5_d'] = nrm((L, S5_WIDTH), 1.0)
    inp['s5_w_glu'] = nrm((L, S5_WIDTH, S5_WIDTH), S5_WIDTH ** -0.5)
    inp['s5_b_glu'] = nrm((L, S5_WIDTH), 0.01)
    inp['nsa_pe_k'] = nrm((L, CMP_BLOCK, HEAD_DIM), 0.02)
    inp['nsa_pe_v'] = nrm((L, CMP_BLOCK, HEAD_DIM), 0.02)
    inp['nsa_ck_w1'] = nrm((L, CMP_BLOCK * HEAD_DIM, CMP_HIDDEN), (CMP_BLOCK * HEAD_DIM) ** -0.5)
    inp['nsa_ck_w2'] = nrm((L, CMP_HIDDEN, HEAD_DIM), CMP_HIDDEN ** -0.5)
    inp['nsa_cv_w1'] = nrm((L, CMP_BLOCK * HEAD_DIM, CMP_HIDDEN), (CMP_BLOCK * HEAD_DIM) ** -0.5)
    inp['nsa_cv_w2'] = nrm((L, CMP_HIDDEN, HEAD_DIM), CMP_HIDDEN ** -0.5)
    inp['gla_w_a2'] = nrm((L, GLA_RANK, GLA_KWIDTH), GLA_RANK ** -0.5)
    inp['gla_b_a'] = nrm((L, GLA_KWIDTH), 0.01)
    inp['gla_norm_g'] = gain((L, GLA_WIDTH))
    inp['w_br_s5'] = nrm((L, S5_WIDTH, D_MODEL), S5_WIDTH ** -0.5)
    inp['w_br_nsa'] = nrm((L, NSA_WIDTH, D_MODEL), NSA_WIDTH ** -0.5)
    inp['w_br_gla'] = nrm((L, GLA_WIDTH, D_MODEL), GLA_WIDTH ** -0.5)
    inp['w_out'] = nrm((L, D_MODEL, D_MODEL), D_MODEL ** -0.5)
    inp['norm2_g'] = gain((L, D_MODEL))
    inp['ffn_w_up'] = nrm((L, D_MODEL, 2 * D_FF), D_MODEL ** -0.5)
    inp['ffn_conv_w'] = nrm((L, CONV_WIDTH, 2 * D_FF), CONV_WIDTH ** -0.5)
    inp['ffn_conv_b'] = nrm((L, 2 * D_FF), 0.01)
    inp['ffn_w_down'] = nrm((L, D_FF, D_MODEL), D_FF ** -0.5)
    inp['final_g'] = gain((D_MODEL,))
    return inp


def reference(x, positions, norm1_g, w_in, s5_lam_re, s5_lam_im, s5_log_dt, s5_b_re, s5_b_im, s5_c_re, s5_c_im, s5_d, s5_w_glu, s5_b_glu, nsa_pe_k, nsa_pe_v, nsa_ck_w1, nsa_ck_w2, nsa_cv_w1, nsa_cv_w2, gla_w_a2, gla_b_a, gla_norm_g, w_br_s5, w_br_nsa, w_br_gla, w_out, norm2_g, ffn_w_up, ffn_conv_w, ffn_conv_b, ffn_w_down, final_g):
    bsz, seq, _ = x.shape
    cos, sin = rotary_tables(positions)
    offsets = np.cumsum(IN_SPLIT)[:-1].tolist()
    for l in range(DEPTH):
        xn = rms_norm(x, norm1_g[l])
        h = xn @ w_in[l]
        (u_s5, q_nsa, kc_in, vc_in, ks_in, vs_in, kw_in, vw_in, g_nsa,
         q_gla, k_gla, v_gla, a_gla, r_gla, g_merge) = jnp.split(h, offsets, axis=-1)
        y_s5 = s5_mixer(u_s5, s5_lam_re[l], s5_lam_im[l], s5_log_dt[l], s5_b_re[l], s5_b_im[l], s5_c_re[l], s5_c_im[l], s5_d[l], s5_w_glu[l], s5_b_glu[l])
        y_nsa = nsa_mixer(q_nsa, kc_in, vc_in, ks_in, vs_in, kw_in, vw_in, g_nsa, cos, sin, nsa_pe_k[l], nsa_pe_v[l], nsa_ck_w1[l], nsa_ck_w2[l], nsa_cv_w1[l], nsa_cv_w2[l])
        y_gla = gla_mixer(q_gla, k_gla, v_gla, a_gla, r_gla, gla_w_a2[l], gla_b_a[l], gla_norm_g[l])
        gm = jax.nn.sigmoid(g_merge.astype(jnp.float32)).reshape(bsz, seq, 3, D_MODEL)
        mixed = (gm[:, :, 0] * (y_s5 @ w_br_s5[l]) + gm[:, :, 1] * (y_nsa @ w_br_nsa[l]) + gm[:, :, 2] * (y_gla @ w_br_gla[l])).astype(x.dtype)
        x = x + mixed @ w_out[l]
        x = x + conv_ffn(rms_norm(x, norm2_g[l]), ffn_w_up[l], ffn_conv_w[l], ffn_conv_b[l], ffn_w_down[l])
    return rms_norm(x, final_g)
```

```python
import functools
import math

import jax
import jax.numpy as jnp
from jax import lax
from jax.experimental import pallas as pl
from jax.experimental.pallas import tpu as pltpu

F32 = jnp.float32
BF16 = jnp.bfloat16
I32 = jnp.int32

EPS = 1e-6
NEG_INF = -1e30

S5_GROUP = 16
S5_STATE = 64
NSA_HEADS = 8
NSA_KV_GROUPS = 2
NSA_HPG = NSA_HEADS // NSA_KV_GROUPS
HEAD_DIM = 64
ROPE_DIM = 16
ROPE_THETA = 500000.0
CMP_BLOCK = 32
CMP_STRIDE = 16
SLC_BLOCK = 64
SLC_TOPK = 16
WINDOW = 512
Q_BLOCK = 128
FORCE_BONUS = 1.0e4
GLA_HEADS = 4
GLA_DK = 32
GLA_DV = 64
GLA_TAU = 16.0
GLA_CHUNK = 64
CONV_WIDTH = 3

LANES = 128
SUBLANES = 8
KV_TILE = 512
VMEM_LIMIT = 56 * 1024 * 1024


def _cparams(sem):
    return pltpu.CompilerParams(dimension_semantics=sem, vmem_limit_bytes=VMEM_LIMIT)


def _bdot(a, b):
    return jnp.dot(a.astype(BF16), b.astype(BF16), preferred_element_type=F32)


def _bdot_nt(a, b):
    return lax.dot_general(a.astype(BF16), b.astype(BF16), (((1,), (1,)), ((), ())),
                           preferred_element_type=F32)


def _hdot(a, b):
    return jnp.dot(a, b, preferred_element_type=F32, precision=lax.Precision.HIGHEST)


def _hdot_nt(a, b):
    return lax.dot_general(a, b, (((1,), (1,)), ((), ())), preferred_element_type=F32,
                           precision=lax.Precision.HIGHEST)


def _hdot_tn(a, b):
    return lax.dot_general(a, b, (((0,), (0,)), ((), ())), preferred_element_type=F32,
                           precision=lax.Precision.HIGHEST)


def _rms(x, g):
    return x * lax.rsqrt(jnp.mean(x * x, axis=-1, keepdims=True) + EPS) * g


def _softmax_axis0(s, keep):
    s = jnp.where(keep, s, NEG_INF)
    m = jnp.max(s, axis=0, keepdims=True)
    p = jnp.where(keep, jnp.exp(s - m), 0.0)
    return p / jnp.maximum(jnp.sum(p, axis=0, keepdims=True), 1e-30)


def _rope_tab_kernel(pos_ref, invf_ref, c_ref, s1_ref, s2_ref):
    ang = pos_ref[...].astype(F32) * invf_ref[...]
    lane = lax.broadcasted_iota(I32, ang.shape, 1) % HEAD_DIM
    cosv = jnp.cos(ang)
    sinv = jnp.sin(ang)
    half = ROPE_DIM // 2
    c_ref[...] = jnp.where(lane < ROPE_DIM, cosv, 1.0)
    s1_ref[...] = jnp.where(lane < half, 0.0, jnp.where(lane < ROPE_DIM, sinv, 0.0))
    s2_ref[...] = jnp.where(lane < half, -sinv, 0.0)


def _rope_tables(positions):
    seq = positions.shape[0]
    tm = min(seq, 2048)
    inv_freq = ROPE_THETA ** (-jnp.arange(0, ROPE_DIM, 2, dtype=F32) / ROPE_DIM)
    lane = jnp.arange(LANES) % HEAD_DIM
    invf = jnp.where(lane < ROPE_DIM, inv_freq[lane % (ROPE_DIM // 2)], 0.0).reshape(1, LANES)
    tab = jax.ShapeDtypeStruct((seq, LANES), F32)
    return pl.pallas_call(
        _rope_tab_kernel,
        out_shape=(tab, tab, tab),
        grid=(seq // tm,),
        in_specs=[pl.BlockSpec((tm, 1), lambda i: (i, 0)),
                  pl.BlockSpec((1, LANES), lambda i: (0, 0))],
        out_specs=tuple(pl.BlockSpec((tm, LANES), lambda i: (i, 0)) for _ in range(3)),
        compiler_params=_cparams(("parallel",)),
        name="rope_tables",
    )(positions.reshape(seq, 1), invf)


def _proj_kernel(x_ref, g_ref, wa_ref, wb_ref, wc_ref, wd_ref, c_ref, s1_ref, s2_ref,
                 oa_ref, ob_ref, oc_ref, od_ref, *, q_width):
    xn = _rms(x_ref[...], g_ref[...]).astype(BF16)
    oa_ref[...] = jnp.dot(xn, wa_ref[...], preferred_element_type=F32)
    oc_ref[...] = jnp.dot(xn, wc_ref[...], preferred_element_type=F32).astype(BF16)
    od_ref[...] = jnp.dot(xn, wd_ref[...], preferred_element_type=F32)
    hb = jnp.dot(xn, wb_ref[...], preferred_element_type=F32)
    c = c_ref[...]
    s1 = s1_ref[...]
    s2 = s2_ref[...]
    half = ROPE_DIM // 2
    for j in range(hb.shape[1] // LANES):
        piece = hb[:, j * LANES:(j + 1) * LANES]
        rot = (piece * c + pltpu.roll(piece, half, 1) * s1
               + pltpu.roll(piece, LANES - half, 1) * s2)
        if j * LANES < q_width:
            rot = rot * (HEAD_DIM ** -0.5)
        ob_ref[:, j * LANES:(j + 1) * LANES] = rot.astype(BF16)


def _project(x, g, wa, wb, wc, wd, tabs, q_width):
    seq, d = x.shape
    tm = min(seq, 512)
    row = lambda w: pl.BlockSpec((tm, w), lambda i: (i, 0))
    full = lambda a: pl.BlockSpec(a.shape, lambda i: (0, 0))
    return pl.pallas_call(
        functools.partial(_proj_kernel, q_width=q_width),
        out_shape=(jax.ShapeDtypeStruct((seq, wa.shape[1]), F32),
                   jax.ShapeDtypeStruct((seq, wb.shape[1]), BF16),
                   jax.ShapeDtypeStruct((seq, wc.shape[1]), BF16),
                   jax.ShapeDtypeStruct((seq, wd.shape[1]), F32)),
        grid=(seq // tm,),
        in_specs=[row(d), full(g), full(wa), full(wb), full(wc), full(wd),
                  row(LANES), row(LANES), row(LANES)],
        out_specs=(row(wa.shape[1]), row(wb.shape[1]), row(wc.shape[1]), row(wd.shape[1])),
        compiler_params=_cparams(("parallel",)),
        name="in_proj",
    )(x, g, wa, wb, wc, wd, *tabs)


def _s5_prep_kernel(lr_ref, li_ref, ldt_ref, btr_ref, bti_ref, ctr_ref, cti_ref,
                    bre_ref, bim_ref, cre_ref, cim_ref, tab_ref):
    lr = lr_ref[...]
    li = li_ref[...]
    dt = jnp.exp(ldt_ref[...])
    mag = jnp.exp(lr * dt)
    ar = mag * jnp.cos(li * dt)
    ai = mag * jnp.sin(li * dt)
    nr = ar - 1.0
    ni = ai
    den = lr * lr + li * li
    f_re = (nr * lr + ni * li) / den
    f_im = (ni * lr - nr * li) / den
    btr = btr_ref[...]
    bti = bti_ref[...]
    rows = lax.broadcasted_iota(I32, btr.shape, 0) // S5_GROUP
    cols = lax.broadcasted_iota(I32, btr.shape, 1) // S5_STATE
    diag = rows == cols
    bre_ref[...] = jnp.where(diag, f_re * btr - f_im * bti, 0.0).astype(BF16)
    bim_ref[...] = jnp.where(diag, f_re * bti + f_im * btr, 0.0).astype(BF16)
    cre_ref[...] = jnp.where(diag, ctr_ref[...], 0.0).astype(BF16)
    cim_ref[...] = jnp.where(diag, cti_ref[...], 0.0).astype(BF16)
    pr = [ar]
    pi = [ai]
    for _ in range(SUBLANES - 1):
        pr_n = pr[-1] * ar - pi[-1] * ai
        pi_n = pr[-1] * ai + pi[-1] * ar
        pr.append(pr_n)
        pi.append(pi_n)
    rid = lax.broadcasted_iota(I32, (SUBLANES, lr.shape[1]), 0)
    zero = jnp.zeros((SUBLANES, lr.shape[1]), F32)
    p_re = zero
    p_im = zero
    for k in range(SUBLANES):
        p_re = jnp.where(rid == k, pr[k], p_re)
        p_im = jnp.where(rid == k, pi[k], p_im)
    for n, k in enumerate((1, 2, 4)):
        tab_ref[2 * n] = jnp.where(rid >= k, pr[k - 1], 0.0)
        tab_ref[2 * n + 1] = jnp.where(rid >= k, pi[k - 1], 0.0)
    tab_ref[6] = p_re
    tab_ref[7] = p_im


def _s5_prep(lam_re, lam_im, log_dt, b_re, b_im, c_re, c_im):
    g, p = lam_re.shape
    h = b_re.shape[-1]
    gp = g * p
    rowv = lambda a: a.reshape(1, gp)
    tile_t = lambda a: jnp.tile(a.reshape(h, gp), (g, 1))
    btr = tile_t(jnp.transpose(b_re, (2, 0, 1)))
    bti = tile_t(jnp.transpose(b_im, (2, 0, 1)))
    ctr = tile_t(jnp.transpose(c_re, (1, 0, 2)))
    cti = tile_t(jnp.transpose(c_im, (1, 0, 2)))
    mat = jax.ShapeDtypeStruct((g * h, gp), BF16)
    return pl.pallas_call(
        _s5_prep_kernel,
        out_shape=(mat, mat, mat, mat, jax.ShapeDtypeStruct((8, SUBLANES, gp), F32)),
        name="s5_prep",
    )(rowv(lam_re), rowv(lam_im), rowv(jnp.repeat(log_dt, p)), btr, bti, ctr, cti)


def _s5_kernel(u_ref, bre_ref, bim_ref, cre_ref, cim_ref, tab_ref, d_ref, wg_ref, bg_ref,
               y_ref, xr_ref, xi_ref, car_ref):
    @pl.when(pl.program_id(0) == 0)
    def _():
        car_ref[...] = jnp.zeros_like(car_ref)

    u = u_ref[...]
    ub = u.astype(BF16)
    xr_ref[...] = jnp.dot(ub, bre_ref[...], preferred_element_type=F32)
    xi_ref[...] = jnp.dot(ub, bim_ref[...], preferred_element_type=F32)
    tb = u.shape[0]

    def slab(r, carry):
        cr, ci = carry
        off = pl.multiple_of(r * SUBLANES, SUBLANES)
        xr = xr_ref[pl.ds(off, SUBLANES), :]
        xi = xi_ref[pl.ds(off, SUBLANES), :]
        for n, k in enumerate((1, 2, 4)):
            tr = tab_ref[2 * n]
            ti = tab_ref[2 * n + 1]
            sr = pltpu.roll(xr, k, 0)
            si = pltpu.roll(xi, k, 0)
            xr, xi = xr + tr * sr - ti * si, xi + tr * si + ti * sr
        pr = tab_ref[6]
        pi = tab_ref[7]
        xr, xi = xr + pr * cr - pi * ci, xi + pr * ci + pi * cr
        xr_ref[pl.ds(off, SUBLANES), :] = xr
        xi_ref[pl.ds(off, SUBLANES), :] = xi
        return xr[SUBLANES - 1:SUBLANES, :], xi[SUBLANES - 1:SUBLANES, :]

    cr, ci = lax.fori_loop(0, tb // SUBLANES, slab, (car_ref[0:1, :], car_ref[1:2, :]))
    car_ref[0:1, :] = cr
    car_ref[1:2, :] = ci
    y = _bdot_nt(xr_ref[...], cre_ref[...]) - _bdot_nt(xi_ref[...], cim_ref[...])
    y = y + d_ref[...] * u
    y = jax.nn.gelu(y)
    y = y * jax.nn.sigmoid(_bdot(y, wg_ref[...]) + bg_ref[...])
    y_ref[...] = y.astype(y_ref.dtype)


def _s5_mixer(u, prep, d_skip, w_glu, b_glu):
    seq, w = u.shape
    bre, bim, cre, cim, tab = prep
    gp = bre.shape[1]
    tb = min(seq, 512)
    full = lambda a: pl.BlockSpec(a.shape, lambda i: (0,) * a.ndim)
    return pl.pallas_call(
        _s5_kernel,
        out_shape=jax.ShapeDtypeStruct((seq, w), BF16),
        grid=(seq // tb,),
        in_specs=[pl.BlockSpec((tb, w), lambda i: (i, 0)), full(bre), full(bim), full(cre),
                  full(cim), full(tab), full(d_skip), full(w_glu), full(b_glu)],
        out_specs=pl.BlockSpec((tb, w), lambda i: (i, 0)),
        scratch_shapes=[pltpu.VMEM((tb, gp), F32), pltpu.VMEM((tb, gp), F32),
                        pltpu.VMEM((SUBLANES, gp), F32)],
        compiler_params=_cparams(("arbitrary",)),
        name="s5_mixer",
    )(u, bre, bim, cre, cim, tab, d_skip, w_glu, b_glu)


def _compress_kernel(r_ref, w1_ref, w2_ref, pe_ref, o_ref):
    r = r_ref[0, 0]
    half = r.shape[1]
    w1 = w1_ref[0]
    a = jnp.dot(r, w1[:half], preferred_element_type=F32)
    b = jnp.dot(r, w1[half:], preferred_element_type=F32)
    pew = _bdot(pe_ref[0], w1)
    nc = r.shape[0]
    hid = a + pltpu.roll(b, nc - 1, 0) + pew[0:1, :]
    o_ref[0, 0] = _bdot(jax.nn.gelu(hid), w2_ref[0]).astype(o_ref.dtype)


def _compress(r, w1, w2, pe):
    two, g, nc, wid = r.shape
    dh = w2.shape[-1]
    return pl.pallas_call(
        _compress_kernel,
        out_shape=jax.ShapeDtypeStruct((two, g, nc, dh), BF16),
        grid=(two, g),
        in_specs=[pl.BlockSpec((1, 1, nc, wid), lambda a, b: (a, b, 0, 0)),
                  pl.BlockSpec((1,) + w1.shape[1:], lambda a, b: (a, 0, 0)),
                  pl.BlockSpec((1,) + w2.shape[1:], lambda a, b: (a, 0, 0)),
                  pl.BlockSpec((1,) + pe.shape[1:], lambda a, b: (a, 0, 0))],
        out_specs=pl.BlockSpec((1, 1, nc, dh), lambda a, b: (a, b, 0, 0)),
        compiler_params=_cparams(("parallel", "parallel")),
        name="compress",
    )(r, w1, w2, pe)


def _nsa_kernel(qt_ref, kc_ref, vct_ref, ks_ref, vst_ref, kw_ref, vwt_ref, g_ref,
                out_ref, imp_ref, sel_ref, *, nc, ns):
    qi = pl.program_id(1)
    qt = qt_ref[0, 0]
    width = qt.shape[1]
    lane = lax.broadcasted_iota(I32, (1, width), 1)
    t_row = qi * Q_BLOCK + (lane % Q_BLOCK)

    s = jnp.dot(kc_ref[0], qt, preferred_element_type=F32)
    ci = lax.broadcasted_iota(I32, (nc, 1), 0)
    p = _softmax_axis0(s, (ci * CMP_STRIDE + (CMP_BLOCK - 1)) <= t_row)
    o_cmp = jnp.dot(vct_ref[0], p.astype(BF16), preferred_element_type=F32)
    imp = p[:, 0:Q_BLOCK]
    for h in range(1, NSA_HPG):
        imp = imp + p[:, h * Q_BLOCK:(h + 1) * Q_BLOCK]

    ratio = SLC_BLOCK // CMP_STRIDE
    span = ratio + CMP_BLOCK // CMP_STRIDE - 1
    pad = SUBLANES
    imp_ref[0:pad, :] = jnp.zeros((pad, Q_BLOCK), F32)
    imp_ref[pad:pad + nc, :] = imp
    imp_ref[pad + nc:pad + nc + pad, :] = jnp.zeros((pad, Q_BLOCK), F32)
    lead = CMP_BLOCK // CMP_STRIDE - 1
    imp_slc = imp_ref[pl.ds(pad - lead, ns, stride=ratio), :]
    for sft in range(1, span):
        imp_slc = imp_slc + imp_ref[pl.ds(pad - lead + sft, ns, stride=ratio), :]

    tq = qi * Q_BLOCK + lax.broadcasted_iota(I32, (1, Q_BLOCK), 1)
    jb = lax.broadcasted_iota(I32, (ns, 1), 0)
    cur = tq // SLC_BLOCK
    forced = (jb == 0) | (jb == cur) | (jb == cur - 1)
    score = jnp.where(jb * SLC_BLOCK <= tq, imp_slc + forced.astype(F32) * FORCE_BONUS, NEG_INF)
    jf = jb.astype(F32)
    sel = jnp.zeros((ns, Q_BLOCK), F32)
    for _ in range(min(SLC_TOPK, ns)):
        mx = jnp.max(score, axis=0, keepdims=True)
        first = jnp.min(jnp.where(score == mx, jf, float(ns)), axis=0, keepdims=True)
        hit = jf == first
        sel = jnp.where(hit, 1.0, sel)
        score = jnp.where(hit, -jnp.inf, score)
    sel_ref[...] = sel

    dh = qt.shape[0]
    blocks = KV_TILE // SLC_BLOCK

    def kv_step(kt, carry, causal):
        m, l, acc = carry
        off = pl.multiple_of(kt * KV_TILE, KV_TILE)
        s = jnp.dot(ks_ref[0, pl.ds(off, KV_TILE), :], qt, preferred_element_type=F32)
        sel_t = sel_ref[pl.ds(pl.multiple_of(kt * blocks, blocks), blocks), :]
        selx = jnp.concatenate(
            [jnp.broadcast_to(sel_t[b:b + 1, :], (SLC_BLOCK, Q_BLOCK)) for b in range(blocks)], axis=0)
        selx = jnp.concatenate([selx] * NSA_HPG, axis=1)
        if causal:
            kpos = off + lax.broadcasted_iota(I32, (KV_TILE, 1), 0)
            selx = jnp.where(kpos <= t_row, selx, 0.0)
        keep = selx > 0.5
        s = jnp.where(keep, s, NEG_INF)
        m_new = jnp.maximum(m, jnp.max(s, axis=0, keepdims=True))
        alpha = jnp.exp(m - m_new)
        pt = jnp.where(keep, jnp.exp(s - m_new), 0.0)
        l = alpha * l + jnp.sum(pt, axis=0, keepdims=True)
        acc = alpha * acc + jnp.dot(vst_ref[0, :, pl.ds(off, KV_TILE)], pt.astype(BF16),
                                    preferred_element_type=F32)
        return m_new, l, acc

    init = (jnp.full((1, width), NEG_INF, F32), jnp.zeros((1, width), F32), jnp.zeros((dh, width), F32))
    n_full = (qi * Q_BLOCK) // KV_TILE
    carry = lax.fori_loop(0, n_full, functools.partial(kv_step, causal=False), init)
    _, l, acc = kv_step(n_full, carry, True)
    o_slc = acc / jnp.maximum(l, 1e-30)

    span_w = WINDOW + Q_BLOCK
    woff = pl.multiple_of(qi * Q_BLOCK, Q_BLOCK)
    s = jnp.dot(kw_ref[0, pl.ds(woff, span_w), :], qt, preferred_element_type=F32)
    kpos = qi * Q_BLOCK - WINDOW + lax.broadcasted_iota(I32, (span_w, 1), 0)
    dpos = t_row - kpos
    p = _softmax_axis0(s, (dpos >= 0) & (dpos < WINDOW) & (kpos >= 0))
    o_win = jnp.dot(vwt_ref[0, :, pl.ds(woff, span_w)], p.astype(BF16), preferred_element_type=F32)

    gate = jax.nn.sigmoid(g_ref[0, 0])
    outs = []
    for h in range(NSA_HPG):
        hs = slice(h * Q_BLOCK, (h + 1) * Q_BLOCK)
        y = (gate[3 * h:3 * h + 1, :] * o_cmp[:, hs] + gate[3 * h + 1:3 * h + 2, :] * o_slc[:, hs]
             + gate[3 * h + 2:3 * h + 3, :] * o_win[:, hs])
        outs.append(y.T)
    out_ref[...] = jnp.concatenate(outs, axis=1).astype(out_ref.dtype)


def _nsa(qt, kc, vct, ks, vst, kw, vwt, gates):
    g, nqt, dh, width = qt.shape
    seq = ks.shape[1]
    nc = kc.shape[1]
    ns = seq // SLC_BLOCK
    per_g = lambda a: pl.BlockSpec((1,) + a.shape[1:], lambda gi, qi: (gi,) + (0,) * (a.ndim - 1))
    return pl.pallas_call(
        functools.partial(_nsa_kernel, nc=nc, ns=ns),
        out_shape=jax.ShapeDtypeStruct((seq, g * NSA_HPG * dh), BF16),
        grid=(g, nqt),
        in_specs=[pl.BlockSpec((1, 1, dh, width), lambda gi, qi: (gi, qi, 0, 0)),
                  per_g(kc), per_g(vct), per_g(ks), per_g(vst), per_g(kw), per_g(vwt),
                  pl.BlockSpec((1, 1) + gates.shape[2:], lambda gi, qi: (qi, gi, 0, 0))],
        out_specs=pl.BlockSpec((Q_BLOCK, NSA_HPG * dh), lambda gi, qi: (qi, gi)),
        scratch_shapes=[pltpu.VMEM((nc + 2 * SUBLANES, Q_BLOCK), F32),
                        pltpu.VMEM((ns, Q_BLOCK), F32)],
        compiler_params=_cparams(("parallel", "arbitrary")),
        name="sparse_attention",
    )(qt, kc, vct, ks, vst, kw, vwt, gates)


def _gla_kernel(h_ref, wa_ref, ba_ref, ng_ref, y_ref, st_ref, la_ref, o_ref):
    @pl.when(pl.program_id(0) == 0)
    def _():
        st_ref[...] = jnp.zeros_like(st_ref)

    kw = GLA_HEADS * GLA_DK
    vw = GLA_HEADS * GLA_DV
    c = GLA_CHUNK
    tb = h_ref.shape[0]
    a_low = h_ref[:, 2 * kw + 2 * vw:2 * kw + 2 * vw + LANES]
    la_ref[...] = jax.nn.log_sigmoid(_hdot(a_low, wa_ref[...]) + ba_ref[...]) / GLA_TAU

    ri = lax.broadcasted_iota(I32, (c, c), 0)
    cj = lax.broadcasted_iota(I32, (c, c), 1)
    tril = ri >= cj
    tril_f = tril.astype(F32)
    ones_c = jnp.ones((c, LANES), F32)
    krow = lax.broadcasted_iota(I32, (kw, vw), 0) // GLA_DK
    vcol = lax.broadcasted_iota(I32, (kw, vw), 1) // GLA_DV
    blockdiag = krow == vcol
    klane = lax.broadcasted_iota(I32, (1, kw), 1) // GLA_DK
    vlane = lax.broadcasted_iota(I32, (1, vw), 1) // GLA_DV
    va = lax.broadcasted_iota(I32, (vw, vw), 0) // GLA_DV
    vb = lax.broadcasted_iota(I32, (vw, vw), 1) // GLA_DV
    head_avg = jnp.where(va == vb, 1.0 / GLA_DV, 0.0)

    def chunk(n, _):
        off = pl.multiple_of(n * c, c)
        rows = pl.ds(off, c)
        q = h_ref[rows, 0:kw] * (GLA_DK ** -0.5)
        k = h_ref[rows, kw:2 * kw]
        v = h_ref[rows, 2 * kw:2 * kw + vw]
        la = la_ref[rows, :]
        bcum = _hdot(tril_f, la)
        b_last = bcum[c - 1:c, :]
        q_t = q * jnp.exp(bcum)
        k_t = k * jnp.exp(-bcum)
        k_d = k * jnp.exp(b_last - bcum)
        state = st_ref[...]
        o = _hdot(q_t, state)
        for hd in range(GLA_HEADS):
            attn = _hdot_nt(jnp.where(klane == hd, q_t, 0.0), k_t)
            attn = jnp.where(tril, attn, 0.0)
            o = o + jnp.where(vlane == hd, _hdot(attn, v), 0.0)
        kv = jnp.where(blockdiag, _hdot_tn(k_d, v), 0.0)
        decay = jnp.exp(_hdot_tn(la, ones_c))
        st_ref[...] = jnp.concatenate([decay] * (vw // LANES), axis=1) * state + kv
        o_ref[rows, :] = o
        return 0

    lax.fori_loop(0, tb // c, chunk, 0)
    o = o_ref[...]
    o = o * lax.rsqrt(_hdot(o * o, head_avg) + EPS)
    r = h_ref[:, 2 * kw + vw:2 * kw + 2 * vw]
    y_ref[...] = (o * ng_ref[...] * (r * jax.nn.sigmoid(r))).astype(y_ref.dtype)


def _gla_mixer(hd, w_a2p, b_a, norm_g):
    seq, wid = hd.shape
    tb = min(seq, 512)
    kw = GLA_HEADS * GLA_DK
    vw = GLA_HEADS * GLA_DV
    full = lambda a: pl.BlockSpec(a.shape, lambda i: (0, 0))
    return pl.pallas_call(
        _gla_kernel,
        out_shape=jax.ShapeDtypeStruct((seq, vw), BF16),
        grid=(seq // tb,),
        in_specs=[pl.BlockSpec((tb, wid), lambda i: (i, 0)), full(w_a2p), full(b_a), full(norm_g)],
        out_specs=pl.BlockSpec((tb, vw), lambda i: (i, 0)),
        scratch_shapes=[pltpu.VMEM((kw, vw), F32), pltpu.VMEM((tb, kw), F32), pltpu.VMEM((tb, vw), F32)],
        compiler_params=_cparams(("arbitrary",)),
        name="gla_mixer",
    )(hd, w_a2p, b_a, norm_g)


def _merge_kernel(x_ref, g_ref, ys_ref, yn_ref, yg_ref, wgm_ref, ws_ref, wn_ref, wg_ref, wo_ref, o_ref):
    x = x_ref[...]
    d = x.shape[1]
    xn = _rms(x, g_ref[...]).astype(BF16)
    gm = jax.nn.sigmoid(jnp.dot(xn, wgm_ref[...], preferred_element_type=F32))
    mixed = (gm[:, 0:d] * jnp.dot(ys_ref[...], ws_ref[...], preferred_element_type=F32)
             + gm[:, d:2 * d] * jnp.dot(yn_ref[...], wn_ref[...], preferred_element_type=F32)
             + gm[:, 2 * d:3 * d] * jnp.dot(yg_ref[...], wg_ref[...], preferred_element_type=F32))
    o_ref[...] = x + _bdot(mixed, wo_ref[...])


def _merge(x, g, ys, yn, yg, wgm, ws, wn, wg, wo):
    seq, d = x.shape
    tm = min(seq, 512)
    row = lambda a: pl.BlockSpec((tm, a.shape[1]), lambda i: (i, 0))
    full = lambda a: pl.BlockSpec(a.shape, lambda i: (0, 0))
    return pl.pallas_call(
        _merge_kernel,
        out_shape=jax.ShapeDtypeStruct((seq, d), F32),
        grid=(seq // tm,),
        in_specs=[row(x), full(g), row(ys), row(yn), row(yg), full(wgm), full(ws), full(wn), full(wg),
                  full(wo)],
        out_specs=row(x),
        compiler_params=_cparams(("parallel",)),
        name="merge",
    )(x, g, ys, yn, yg, wgm, ws, wn, wg, wo)


def _ffn_kernel(x_ref, xp_ref, g_ref, wug_ref, wuv_ref, cwg_ref, cwv_ref, cbg_ref, cbv_ref, wd_ref,
                fg_ref, o_ref, xn_ref, *, final):
    i = pl.program_id(0)
    j = pl.program_id(1)
    tm = x_ref.shape[0]
    halo = xp_ref.shape[0]

    @pl.when(j == 0)
    def _():
        xn_ref[0:halo, :] = jnp.where(i == 0, 0.0, _rms(xp_ref[...], g_ref[...])).astype(BF16)
        xn_ref[halo:halo + tm, :] = _rms(x_ref[...], g_ref[...]).astype(BF16)

    xn = xn_ref[...]

    def conv(w_ref, cw_ref, cb_ref):
        h = jnp.dot(xn, w_ref[...], preferred_element_type=F32)
        hc = cb_ref[...]
        for t in range(CONV_WIDTH):
            sh = CONV_WIDTH - 1 - t
            hs = h if sh == 0 else pltpu.roll(h, sh, 0)
            hc = hc + cw_ref[t:t + 1, :] * hs[halo:halo + tm, :]
        return hc

    act = jax.nn.gelu(conv(wug_ref, cwg_ref, cbg_ref)) * conv(wuv_ref, cwv_ref, cbv_ref)
    part = _bdot(act, wd_ref[...])

    @pl.when(j == 0)
    def _():
        o_ref[...] = x_ref[...] + part

    @pl.when(j > 0)
    def _():
        o_ref[...] = o_ref[...] + part

    if final:
        @pl.when(j == pl.num_programs(1) - 1)
        def _():
            o_ref[...] = _rms(o_ref[...], fg_ref[...])


def _ffn(x, g, w_up, conv_w, conv_b, w_down, final_g, final):
    seq, d = x.shape
    dff = w_down.shape[0]
    tm = min(seq, 512)
    nj = 2
    tn = dff // nj
    halo = SUBLANES
    hb = tm // halo
    return pl.pallas_call(
        functools.partial(_ffn_kernel, final=final),
        out_shape=jax.ShapeDtypeStruct((seq, d), F32),
        grid=(seq // tm, nj),
        in_specs=[pl.BlockSpec((tm, d), lambda i, j: (i, 0)),
                  pl.BlockSpec((halo, d), lambda i, j: (jnp.maximum(i * hb - 1, 0), 0)),
                  pl.BlockSpec((1, d), lambda i, j: (0, 0)),
                  pl.BlockSpec((d, tn), lambda i, j: (0, j)),
                  pl.BlockSpec((d, tn), lambda i, j: (0, nj + j)),
                  pl.BlockSpec((CONV_WIDTH, tn), lambda i, j: (0, j)),
                  pl.BlockSpec((CONV_WIDTH, tn), lambda i, j: (0, nj + j)),
                  pl.BlockSpec((1, tn), lambda i, j: (0, j)),
                  pl.BlockSpec((1, tn), lambda i, j: (0, nj + j)),
                  pl.BlockSpec((tn, d), lambda i, j: (j, 0)),
                  pl.BlockSpec((1, d), lambda i, j: (0, 0))],
        out_specs=pl.BlockSpec((tm, d), lambda i, j: (i, 0)),
        scratch_shapes=[pltpu.VMEM((halo + tm, d), BF16)],
        compiler_params=_cparams(("parallel", "arbitrary")),
        name="conv_ffn",
    )(x, x, g, w_up, w_up, conv_w, conv_w, conv_b, conv_b, w_down, final_g)


def _split_w_in(w):
    d = w.shape[0]
    s5w = d // 4
    nsa_w = NSA_HEADS * HEAD_DIM
    kvw = NSA_KV_GROUPS * HEAD_DIM
    kw = GLA_HEADS * GLA_DK
    vw = GLA_HEADS * GLA_DV
    sizes = (s5w, nsa_w, kvw, kvw, kvw, kvw, kvw, kvw, 3 * NSA_HEADS, kw, kw, vw, 16, vw, 3 * d)
    assert sum(sizes) == w.shape[1]
    offs = [0]
    for sz in sizes:
        offs.append(offs[-1] + sz)
    col = lambda n: w[:, offs[n]:offs[n + 1]]
    (u, q, kc, vc, ks, vs, kwn, vwn, gn, gq, gk, gv, ga, gr, gm) = [col(n) for n in range(len(sizes))]
    padl = lambda a: jnp.pad(a, ((0, 0), (0, LANES - a.shape[1])))
    wa = u
    wb = jnp.concatenate([q, kc, ks, kwn], axis=1)
    wc = jnp.concatenate([vc, vs, vwn], axis=1)
    wd = jnp.concatenate([gq, gk, gv, gr, padl(ga), padl(gn)], axis=1)
    return [a.astype(BF16) for a in (wa, wb, wc, wd, gm)]


def kernel(x, positions, norm1_g, w_in, s5_lam_re, s5_lam_im, s5_log_dt, s5_b_re, s5_b_im, s5_c_re, s5_c_im, s5_d, s5_w_glu, s5_b_glu, nsa_pe_k, nsa_pe_v, nsa_ck_w1, nsa_ck_w2, nsa_cv_w1, nsa_cv_w2, gla_w_a2, gla_b_a, gla_norm_g, w_br_s5, w_br_nsa, w_br_gla, w_out, norm2_g, ffn_w_up, ffn_conv_w, ffn_conv_b, ffn_w_down, final_g):
    bsz, seq, d = x.shape
    depth = w_in.shape[0]
    assert bsz == 1 and seq % KV_TILE == 0 and d % LANES == 0
    g = NSA_KV_GROUPS
    dh = HEAD_DIM
    nqt = seq // Q_BLOCK
    nc = seq // CMP_STRIDE
    nsa_w = NSA_HEADS * dh
    kvw = g * dh
    kw = GLA_HEADS * GLA_DK
    vw = GLA_HEADS * GLA_DV
    row = lambda a: a.reshape(1, -1)

    xs = x.reshape(seq, d)
    tabs = _rope_tables(positions.reshape(seq))

    for l in range(depth):
        wa, wb, wc, wd, wgm = _split_w_in(w_in[l])
        g1 = row(norm1_g[l])
        u_s5, hb, hc, hd = _project(xs, g1, wa, wb, wc, wd, tabs, nsa_w)

        prep = _s5_prep(s5_lam_re[l], s5_lam_im[l], s5_log_dt[l], s5_b_re[l], s5_b_im[l],
                        s5_c_re[l], s5_c_im[l])
        y_s5 = _s5_mixer(u_s5, prep, row(s5_d[l]), s5_w_glu[l].astype(BF16), row(s5_b_glu[l]))

        q = hb[:, :nsa_w]
        qt = q.reshape(nqt, Q_BLOCK, g, NSA_HPG, dh).transpose(2, 0, 4, 3, 1).reshape(g, nqt, dh, NSA_HPG * Q_BLOCK)
        per_group = lambda a: a.reshape(seq, g, dh).transpose(1, 0, 2)
        to_rows = lambda a: per_group(a).reshape(g, nc, CMP_STRIDE * dh)
        k_cmp, k_slc, k_win = (hb[:, nsa_w + n * kvw:nsa_w + (n + 1) * kvw] for n in range(3))
        v_cmp, v_slc, v_win = (hc[:, n * kvw:(n + 1) * kvw] for n in range(3))
        r = jnp.stack([to_rows(k_cmp), to_rows(v_cmp)])
        pe = jnp.stack([nsa_pe_k[l].reshape(1, -1), nsa_pe_v[l].reshape(1, -1)])
        pe = jnp.broadcast_to(pe, (2, SUBLANES, pe.shape[-1]))
        w1 = jnp.stack([nsa_ck_w1[l], nsa_cv_w1[l]]).astype(BF16)
        w2 = jnp.stack([nsa_ck_w2[l], nsa_cv_w2[l]]).astype(BF16)
        cmp = _compress(r, w1, w2, pe)
        kc = cmp[0]
        vct = cmp[1].transpose(0, 2, 1)
        ks = per_group(k_slc)
        vst = per_group(v_slc).transpose(0, 2, 1)
        kwn = jnp.pad(per_group(k_win), ((0, 0), (WINDOW, 0), (0, 0)))
        vwt = jnp.pad(per_group(v_win), ((0, 0), (WINDOW, 0), (0, 0))).transpose(0, 2, 1)
        gates = hd[:, 2 * kw + 2 * vw + LANES:2 * kw + 2 * vw + LANES + 3 * NSA_HEADS]
        gates = gates.reshape(nqt, Q_BLOCK, g, 3 * NSA_HPG).transpose(0, 2, 3, 1)
        y_nsa = _nsa(qt, kc, vct, ks, vst, kwn, vwt, gates)

        w_a2p = jnp.pad(gla_w_a2[l], ((0, LANES - gla_w_a2.shape[1]), (0, 0)))
        y_gla = _gla_mixer(hd, w_a2p, row(gla_b_a[l]), row(gla_norm_g[l]))

        xs = _merge(xs, g1, y_s5, y_nsa, y_gla, wgm, w_br_s5[l].astype(BF16), w_br_nsa[l].astype(BF16),
                    w_br_gla[l].astype(BF16), w_out[l].astype(BF16))
        xs = _ffn(xs, row(norm2_g[l]), ffn_w_up[l].astype(BF16), ffn_conv_w[l], row(ffn_conv_b[l]),
                  ffn_w_down[l].astype(BF16), row(final_g), final=(l == depth - 1))
    return xs.reshape(bsz, seq, d)
```

```python
import functools
import math

import jax
import jax.numpy as jnp
from jax import lax
from jax.experimental import pallas as pl
from jax.experimental.pallas import tpu as pltpu

F32 = jnp.float32
BF16 = jnp.bfloat16
I32 = jnp.int32

EPS = 1e-6
NEG_INF = -1e30

S5_GROUP = 16
S5_STATE = 64
NSA_HEADS = 8
NSA_KV_GROUPS = 2
NSA_HPG = NSA_HEADS // NSA_KV_GROUPS
HEAD_DIM = 64
ROPE_DIM = 16
ROPE_THETA = 500000.0
CMP_BLOCK = 32
CMP_STRIDE = 16
SLC_BLOCK = 64
SLC_TOPK = 16
WINDOW = 512
Q_BLOCK = 128
FORCE_BONUS = 1.0e4
GLA_HEADS = 4
GLA_DK = 32
GLA_DV = 64
GLA_TAU = 16.0
GLA_CHUNK = 64
CONV_WIDTH = 3

LANES = 128
SUBLANES = 8
KV_TILE = 512
CMP_CHUNK = 256
BF16_SUBLANES = 16
VMEM_LIMIT = 56 * 1024 * 1024


def _cparams(sem):
    return pltpu.CompilerParams(dimension_semantics=sem, vmem_limit_bytes=VMEM_LIMIT)


def _bdot(a, b):
    return jnp.dot(a.astype(BF16), b.astype(BF16), preferred_element_type=F32)


def _bdot_nt(a, b):
    return lax.dot_general(a.astype(BF16), b.astype(BF16), (((1,), (1,)), ((), ())),
                           preferred_element_type=F32)


def _hdot(a, b):
    return jnp.dot(a, b, preferred_element_type=F32, precision=lax.Precision.HIGHEST)


def _hdot_nt(a, b):
    return lax.dot_general(a, b, (((1,), (1,)), ((), ())), preferred_element_type=F32,
                           precision=lax.Precision.HIGHEST)


def _hdot_tn(a, b):
    return lax.dot_general(a, b, (((0,), (0,)), ((), ())), preferred_element_type=F32,
                           precision=lax.Precision.HIGHEST)


def _rms(x, g):
    return x * lax.rsqrt(jnp.mean(x * x, axis=-1, keepdims=True) + EPS) * g


def _softmax_axis0(s, keep):
    s = jnp.where(keep, s, NEG_INF)
    m = jnp.max(s, axis=0, keepdims=True)
    p = jnp.where(keep, jnp.exp(s - m), 0.0)
    return p / jnp.maximum(jnp.sum(p, axis=0, keepdims=True), 1e-30)


def _rope_tab_kernel(pos_ref, invf_ref, c_ref, s1_ref, s2_ref):
    ang = pos_ref[...].astype(F32) * invf_ref[...]
    lane = lax.broadcasted_iota(I32, ang.shape, 1) % HEAD_DIM
    cosv = jnp.cos(ang)
    sinv = jnp.sin(ang)
    half = ROPE_DIM // 2
    c_ref[...] = jnp.where(lane < ROPE_DIM, cosv, 1.0)
    s1_ref[...] = jnp.where(lane < half, 0.0, jnp.where(lane < ROPE_DIM, sinv, 0.0))
    s2_ref[...] = jnp.where(lane < half, -sinv, 0.0)


def _rope_tables(positions):
    seq = positions.shape[0]
    tm = min(seq, 2048)
    inv_freq = ROPE_THETA ** (-jnp.arange(0, ROPE_DIM, 2, dtype=F32) / ROPE_DIM)
    lane = jnp.arange(LANES) % HEAD_DIM
    invf = jnp.where(lane < ROPE_DIM, inv_freq[lane % (ROPE_DIM // 2)], 0.0).reshape(1, LANES)
    tab = jax.ShapeDtypeStruct((seq, LANES), F32)
    return pl.pallas_call(
        _rope_tab_kernel,
        out_shape=(tab, tab, tab),
        grid=(seq // tm,),
        in_specs=[pl.BlockSpec((tm, 1), lambda i: (i, 0)),
                  pl.BlockSpec((1, LANES), lambda i: (0, 0))],
        out_specs=tuple(pl.BlockSpec((tm, LANES), lambda i: (i, 0)) for _ in range(3)),
        compiler_params=_cparams(("parallel",)),
        name="rope_tables",
    )(positions.reshape(seq, 1), invf)


def _proj_kernel(x_ref, g_ref, wa_ref, wb_ref, wc_ref, wd_ref, c_ref, s1_ref, s2_ref,
                 oa_ref, ob_ref, oc_ref, od_ref, *, q_width):
    xn = _rms(x_ref[...], g_ref[...]).astype(BF16)
    oa_ref[...] = jnp.dot(xn, wa_ref[...], preferred_element_type=F32)
    oc_ref[...] = jnp.dot(xn, wc_ref[...], preferred_element_type=F32).astype(BF16)
    od_ref[...] = jnp.dot(xn, wd_ref[...], preferred_element_type=F32)
    hb = jnp.dot(xn, wb_ref[...], preferred_element_type=F32)
    c = c_ref[...]
    s1 = s1_ref[...]
    s2 = s2_ref[...]
    half = ROPE_DIM // 2
    for j in range(hb.shape[1] // LANES):
        piece = hb[:, j * LANES:(j + 1) * LANES]
        rot = (piece * c + pltpu.roll(piece, half, 1) * s1
               + pltpu.roll(piece, LANES - half, 1) * s2)
        if j * LANES < q_width:
            rot = rot * (HEAD_DIM ** -0.5 * math.log2(math.e))
        ob_ref[:, j * LANES:(j + 1) * LANES] = rot.astype(BF16)


def _project(x, g, wa, wb, wc, wd, tabs, q_width):
    seq, d = x.shape
    tm = min(seq, 512)
    row = lambda w: pl.BlockSpec((tm, w), lambda i: (i, 0))
    full = lambda a: pl.BlockSpec(a.shape, lambda i: (0, 0))
    return pl.pallas_call(
        functools.partial(_proj_kernel, q_width=q_width),
        out_shape=(jax.ShapeDtypeStruct((seq, wa.shape[1]), F32),
                   jax.ShapeDtypeStruct((seq, wb.shape[1]), BF16),
                   jax.ShapeDtypeStruct((seq, wc.shape[1]), BF16),
                   jax.ShapeDtypeStruct((seq, wd.shape[1]), F32)),
        grid=(seq // tm,),
        in_specs=[row(d), full(g), full(wa), full(wb), full(wc), full(wd),
                  row(LANES), row(LANES), row(LANES)],
        out_specs=(row(wa.shape[1]), row(wb.shape[1]), row(wc.shape[1]), row(wd.shape[1])),
        compiler_params=_cparams(("parallel",)),
        name="in_proj",
    )(x, g, wa, wb, wc, wd, *tabs)


def _s5_prep_kernel(lr_ref, li_ref, ldt_ref, btr_ref, bti_ref, ctr_ref, cti_ref,
                    bre_ref, bim_ref, cre_ref, cim_ref, tab_ref):
    lr = lr_ref[...]
    li = li_ref[...]
    dt = jnp.exp(ldt_ref[...])
    mag = jnp.exp(lr * dt)
    ar = mag * jnp.cos(li * dt)
    ai = mag * jnp.sin(li * dt)
    nr = ar - 1.0
    ni = ai
    den = lr * lr + li * li
    f_re = (nr * lr + ni * li) / den
    f_im = (ni * lr - nr * li) / den
    btr = btr_ref[...]
    bti = bti_ref[...]
    rows = lax.broadcasted_iota(I32, btr.shape, 0) // S5_GROUP
    cols = lax.broadcasted_iota(I32, btr.shape, 1) // S5_STATE
    diag = rows == cols
    bre_ref[...] = jnp.where(diag, f_re * btr - f_im * bti, 0.0).astype(BF16)
    bim_ref[...] = jnp.where(diag, f_re * bti + f_im * btr, 0.0).astype(BF16)
    cre_ref[...] = jnp.where(diag, ctr_ref[...], 0.0).astype(BF16)
    cim_ref[...] = jnp.where(diag, cti_ref[...], 0.0).astype(BF16)
    pr = [ar]
    pi = [ai]
    for _ in range(SUBLANES - 1):
        pr_n = pr[-1] * ar - pi[-1] * ai
        pi_n = pr[-1] * ai + pi[-1] * ar
        pr.append(pr_n)
        pi.append(pi_n)
    rid = lax.broadcasted_iota(I32, (SUBLANES, lr.shape[1]), 0)
    zero = jnp.zeros((SUBLANES, lr.shape[1]), F32)
    p_re = zero
    p_im = zero
    for k in range(SUBLANES):
        p_re = jnp.where(rid == k, pr[k], p_re)
        p_im = jnp.where(rid == k, pi[k], p_im)
    for n, k in enumerate((1, 2, 4)):
        tab_ref[2 * n] = jnp.where(rid >= k, pr[k - 1], 0.0)
        tab_ref[2 * n + 1] = jnp.where(rid >= k, pi[k - 1], 0.0)
    tab_ref[6] = p_re
    tab_ref[7] = p_im


def _s5_prep(lam_re, lam_im, log_dt, b_re, b_im, c_re, c_im):
    g, p = lam_re.shape
    h = b_re.shape[-1]
    gp = g * p
    rowv = lambda a: a.reshape(1, gp)
    tile_t = lambda a: jnp.tile(a.reshape(h, gp), (g, 1))
    btr = tile_t(jnp.transpose(b_re, (2, 0, 1)))
    bti = tile_t(jnp.transpose(b_im, (2, 0, 1)))
    ctr = tile_t(jnp.transpose(c_re, (1, 0, 2)))
    cti = tile_t(jnp.transpose(c_im, (1, 0, 2)))
    mat = jax.ShapeDtypeStruct((g * h, gp), BF16)
    return pl.pallas_call(
        _s5_prep_kernel,
        out_shape=(mat, mat, mat, mat, jax.ShapeDtypeStruct((8, SUBLANES, gp), F32)),
        name="s5_prep",
    )(rowv(lam_re), rowv(lam_im), rowv(jnp.repeat(log_dt, p)), btr, bti, ctr, cti)


def _s5_kernel(u_ref, bre_ref, bim_ref, cre_ref, cim_ref, tab_ref, d_ref, wg_ref, bg_ref,
               y_ref, xr_ref, xi_ref, car_ref):
    @pl.when(pl.program_id(0) == 0)
    def _():
        car_ref[...] = jnp.zeros_like(car_ref)

    u = u_ref[...]
    ub = u.astype(BF16)
    xr_ref[...] = jnp.dot(ub, bre_ref[...], preferred_element_type=F32)
    xi_ref[...] = jnp.dot(ub, bim_ref[...], preferred_element_type=F32)
    tb = u.shape[0]

    def slab(r, carry):
        cr, ci = carry
        off = pl.multiple_of(r * SUBLANES, SUBLANES)
        xr = xr_ref[pl.ds(off, SUBLANES), :]
        xi = xi_ref[pl.ds(off, SUBLANES), :]
        for n, k in enumerate((1, 2, 4)):
            tr = tab_ref[2 * n]
            ti = tab_ref[2 * n + 1]
            sr = pltpu.roll(xr, k, 0)
            si = pltpu.roll(xi, k, 0)
            xr, xi = xr + tr * sr - ti * si, xi + tr * si + ti * sr
        pr = tab_ref[6]
        pi = tab_ref[7]
        xr, xi = xr + pr * cr - pi * ci, xi + pr * ci + pi * cr
        xr_ref[pl.ds(off, SUBLANES), :] = xr
        xi_ref[pl.ds(off, SUBLANES), :] = xi
        return xr[SUBLANES - 1:SUBLANES, :], xi[SUBLANES - 1:SUBLANES, :]

    cr, ci = lax.fori_loop(0, tb // SUBLANES, slab, (car_ref[0:1, :], car_ref[1:2, :]))
    car_ref[0:1, :] = cr
    car_ref[1:2, :] = ci
    y = _bdot_nt(xr_ref[...], cre_ref[...]) - _bdot_nt(xi_ref[...], cim_ref[...])
    y = y + d_ref[...] * u
    y = jax.nn.gelu(y)
    y = y * jax.nn.sigmoid(_bdot(y, wg_ref[...]) + bg_ref[...])
    y_ref[...] = y.astype(y_ref.dtype)


def _s5_mixer(u, prep, d_skip, w_glu, b_glu):
    seq, w = u.shape
    bre, bim, cre, cim, tab = prep
    gp = bre.shape[1]
    tb = min(seq, 512)
    full = lambda a: pl.BlockSpec(a.shape, lambda i: (0,) * a.ndim)
    return pl.pallas_call(
        _s5_kernel,
        out_shape=jax.ShapeDtypeStruct((seq, w), BF16),
        grid=(seq // tb,),
        in_specs=[pl.BlockSpec((tb, w), lambda i: (i, 0)), full(bre), full(bim), full(cre),
                  full(cim), full(tab), full(d_skip), full(w_glu), full(b_glu)],
        out_specs=pl.BlockSpec((tb, w), lambda i: (i, 0)),
        scratch_shapes=[pltpu.VMEM((tb, gp), F32), pltpu.VMEM((tb, gp), F32),
                        pltpu.VMEM((SUBLANES, gp), F32)],
        compiler_params=_cparams(("arbitrary",)),
        name="s5_mixer",
    )(u, bre, bim, cre, cim, tab, d_skip, w_glu, b_glu)


def _compress_kernel(r_ref, w1_ref, w2_ref, pe_ref, o_ref):
    r = r_ref[0, 0]
    half = r.shape[1]
    w1 = w1_ref[0]
    a = jnp.dot(r, w1[:half], preferred_element_type=F32)
    b = jnp.dot(r, w1[half:], preferred_element_type=F32)
    pew = _bdot(pe_ref[0], w1)
    nc = r.shape[0]
    hid = a + pltpu.roll(b, nc - 1, 0) + pew[0:1, :]
    o_ref[0, 0] = _bdot(jax.nn.gelu(hid), w2_ref[0]).astype(o_ref.dtype)


def _compress(r, w1, w2, pe):
    two, g, nc, wid = r.shape
    dh = w2.shape[-1]
    return pl.pallas_call(
        _compress_kernel,
        out_shape=jax.ShapeDtypeStruct((two, g, nc, dh), BF16),
        grid=(two, g),
        in_specs=[pl.BlockSpec((1, 1, nc, wid), lambda a, b: (a, b, 0, 0)),
                  pl.BlockSpec((1,) + w1.shape[1:], lambda a, b: (a, 0, 0)),
                  pl.BlockSpec((1,) + w2.shape[1:], lambda a, b: (a, 0, 0)),
                  pl.BlockSpec((1,) + pe.shape[1:], lambda a, b: (a, 0, 0))],
        out_specs=pl.BlockSpec((1, 1, nc, dh), lambda a, b: (a, b, 0, 0)),
        compiler_params=_cparams(("parallel", "parallel")),
        name="compress",
    )(r, w1, w2, pe)


def _nsa_kernel(qt_ref, kc_ref, vct_ref, ks_ref, vst_ref, kw_ref, vwt_ref, g_ref,
                out_ref, sc_ref, imp_ref, selb_ref, sa_ref, sb_ref, *, nc, ns):
    qi = pl.program_id(1)
    qt = qt_ref[0, 0]
    dh, width = qt.shape
    vrows = vst_ref.shape[1]
    lane = lax.broadcasted_iota(I32, (1, width), 1)
    t_row = qi * Q_BLOCK + (lane % Q_BLOCK)
    tq = qi * Q_BLOCK + lax.broadcasted_iota(I32, (1, Q_BLOCK), 1)
    all_heads = lambda a: jnp.concatenate([a] * NSA_HPG, axis=1)

    pad = SUBLANES
    ch = min(CMP_CHUNK, nc)
    n_ch = (((qi + 1) * Q_BLOCK - CMP_BLOCK) // CMP_STRIDE) // ch + 1

    def cmp_scores(c, m):
        off = pl.multiple_of(c * ch, ch)
        s = jnp.dot(kc_ref[0, pl.ds(off, ch), :], qt, preferred_element_type=F32)
        ci = off + lax.broadcasted_iota(I32, (ch, 1), 0)
        s = jnp.where(ci * CMP_STRIDE + (CMP_BLOCK - 1) <= t_row, s, NEG_INF)
        sc_ref[pl.ds(off, ch), :] = s
        return jnp.maximum(m, jnp.max(s, axis=0, keepdims=True))

    m = lax.fori_loop(0, n_ch, cmp_scores, jnp.full((1, width), NEG_INF, F32))
    m = jnp.maximum(m, 0.1 * NEG_INF)

    def cmp_probs(c, acc):
        off = pl.multiple_of(c * ch, ch)
        e = jnp.exp2(sc_ref[pl.ds(off, ch), :] - m)
        sc_ref[pl.ds(off, ch), :] = e
        return acc + jnp.dot(vct_ref[0, :, pl.ds(off, ch)], e.astype(BF16), preferred_element_type=F32)

    acc = lax.fori_loop(0, n_ch, cmp_probs, jnp.zeros((vrows, width), F32))
    inv_l = 1.0 / jnp.maximum(acc[dh:dh + 1, :], 1e-30)
    o_cmp = acc[0:dh, :] * inv_l
    imp_ref[...] = jnp.zeros_like(imp_ref)

    def cmp_importance(c, _):
        off = pl.multiple_of(c * ch, ch)
        p = sc_ref[pl.ds(off, ch), :] * inv_l
        imp = p[:, 0:Q_BLOCK]
        for h in range(1, NSA_HPG):
            imp = imp + p[:, h * Q_BLOCK:(h + 1) * Q_BLOCK]
        imp_ref[pl.ds(pl.multiple_of(pad + off, pad), ch), :] = imp
        return 0

    lax.fori_loop(0, n_ch, cmp_importance, 0)

    ratio = SLC_BLOCK // CMP_STRIDE
    span = ratio + CMP_BLOCK // CMP_STRIDE - 1
    lead = CMP_BLOCK // CMP_STRIDE - 1
    imp_slc = imp_ref[pl.ds(pad - lead, ns, stride=ratio), :]
    for sft in range(1, span):
        imp_slc = imp_slc + imp_ref[pl.ds(pad - lead + sft, ns, stride=ratio), :]

    jb = lax.broadcasted_iota(I32, (ns, 1), 0)
    cur = tq // SLC_BLOCK
    forced = (jb == 0) | (jb == cur) | (jb == cur - 1)
    score = jnp.where(jb * SLC_BLOCK <= tq, imp_slc + forced.astype(F32) * FORCE_BONUS, NEG_INF)
    jf = jb.astype(F32)
    selb = jnp.full((ns, Q_BLOCK), NEG_INF, F32)
    for _ in range(min(SLC_TOPK, ns)):
        mx = jnp.max(score, axis=0, keepdims=True)
        first = jnp.min(jnp.where(score == mx, jf, float(ns)), axis=0, keepdims=True)
        hit = jf == first
        selb = jnp.where(hit, 0.0, selb)
        score = jnp.where(hit, -jnp.inf, score)
    selb_ref[...] = selb

    blocks = KV_TILE // SLC_BLOCK

    def scores(kt, dst_ref):
        off = pl.multiple_of(kt * KV_TILE, KV_TILE)
        s = jnp.dot(ks_ref[0, pl.ds(off, KV_TILE), :], qt, preferred_element_type=F32)
        bias_t = selb_ref[pl.ds(pl.multiple_of(kt * blocks, blocks), blocks), :]
        bias = jnp.concatenate(
            [jnp.broadcast_to(bias_t[b:b + 1, :], (SLC_BLOCK, Q_BLOCK)) for b in range(blocks)], axis=0)
        kpos = off + lax.broadcasted_iota(I32, (KV_TILE, 1), 0)
        bias = jnp.where(kpos <= tq, bias, NEG_INF)
        dst_ref[...] = s + all_heads(bias)

    def consume(kt, src_ref, m, acc):
        off = pl.multiple_of(kt * KV_TILE, KV_TILE)
        s = src_ref[...]
        m_new = jnp.maximum(m, jnp.max(s, axis=0, keepdims=True))
        p = jnp.exp2(s - m_new).astype(BF16)
        acc = jnp.exp2(m - m_new) * acc + jnp.dot(vst_ref[0, :, pl.ds(off, KV_TILE)], p,
                                                  preferred_element_type=F32)
        return m_new, acc

    n_tiles = (qi * Q_BLOCK) // KV_TILE + 1
    last_tile = ks_ref.shape[1] // KV_TILE - 1
    scores(0, sa_ref)

    def tile_pair(i, carry):
        m, acc = carry
        scores(2 * i + 1, sb_ref)
        m, acc = consume(2 * i, sa_ref, m, acc)
        scores(jnp.minimum(2 * i + 2, last_tile), sa_ref)
        return consume(2 * i + 1, sb_ref, m, acc)

    init = (jnp.full((1, width), NEG_INF, F32), jnp.zeros((vrows, width), F32))
    _, acc = lax.fori_loop(0, (n_tiles + 1) // 2, tile_pair, init)
    o_slc = acc[0:dh, :] / jnp.maximum(acc[dh:dh + 1, :], 1e-30)

    span_w = WINDOW + Q_BLOCK
    woff = pl.multiple_of(qi * Q_BLOCK, Q_BLOCK)
    s = jnp.dot(kw_ref[0, pl.ds(woff, span_w), :], qt, preferred_element_type=F32)
    r = lax.broadcasted_iota(I32, (span_w, 1), 0)
    dpos = lax.broadcasted_iota(I32, (1, Q_BLOCK), 1) + WINDOW - r
    ok = (dpos >= 0) & (dpos < WINDOW) & (qi * Q_BLOCK - WINDOW + r >= 0)
    s = s + all_heads(jnp.where(ok, 0.0, NEG_INF))
    p = jnp.exp2(s - jnp.max(s, axis=0, keepdims=True)).astype(BF16)
    acc = jnp.dot(vwt_ref[0, :, pl.ds(woff, span_w)], p, preferred_element_type=F32)
    o_win = acc[0:dh, :] / jnp.maximum(acc[dh:dh + 1, :], 1e-30)

    gate = jax.nn.sigmoid(g_ref[0, 0])
    outs = []
    for h in range(NSA_HPG):
        hs = slice(h * Q_BLOCK, (h + 1) * Q_BLOCK)
        y = (gate[3 * h:3 * h + 1, :] * o_cmp[:, hs] + gate[3 * h + 1:3 * h + 2, :] * o_slc[:, hs]
             + gate[3 * h + 2:3 * h + 3, :] * o_win[:, hs])
        outs.append(y.T)
    out_ref[...] = jnp.concatenate(outs, axis=1).astype(out_ref.dtype)


def _nsa(qt, kc, vct, ks, vst, kw, vwt, gates):
    g, nqt, dh, width = qt.shape
    seq = ks.shape[1]
    nc = kc.shape[1]
    ns = seq // SLC_BLOCK
    per_g = lambda a: pl.BlockSpec((1,) + a.shape[1:], lambda gi, qi: (gi,) + (0,) * (a.ndim - 1))
    return pl.pallas_call(
        functools.partial(_nsa_kernel, nc=nc, ns=ns),
        out_shape=jax.ShapeDtypeStruct((seq, g * NSA_HPG * dh), BF16),
        grid=(g, nqt),
        in_specs=[pl.BlockSpec((1, 1, dh, width), lambda gi, qi: (gi, qi, 0, 0)),
                  per_g(kc), per_g(vct), per_g(ks), per_g(vst), per_g(kw), per_g(vwt),
                  pl.BlockSpec((1, 1) + gates.shape[2:], lambda gi, qi: (qi, gi, 0, 0))],
        out_specs=pl.BlockSpec((Q_BLOCK, NSA_HPG * dh), lambda gi, qi: (qi, gi)),
        scratch_shapes=[pltpu.VMEM((nc, width), F32),
                        pltpu.VMEM((nc + 2 * SUBLANES, Q_BLOCK), F32),
                        pltpu.VMEM((ns, Q_BLOCK), F32),
                        pltpu.VMEM((KV_TILE, width), F32),
                        pltpu.VMEM((KV_TILE, width), F32)],
        compiler_params=_cparams(("parallel", "arbitrary")),
        name="sparse_attention",
    )(qt, kc, vct, ks, vst, kw, vwt, gates)


def _gla_kernel(h_ref, wa_ref, ba_ref, ng_ref, y_ref, st_ref, la_ref, o_ref):
    @pl.when(pl.program_id(0) == 0)
    def _():
        st_ref[...] = jnp.zeros_like(st_ref)

    kw = GLA_HEADS * GLA_DK
    vw = GLA_HEADS * GLA_DV
    c = GLA_CHUNK
    tb = h_ref.shape[0]
    a_low = h_ref[:, 2 * kw + 2 * vw:2 * kw + 2 * vw + LANES]
    la_ref[...] = jax.nn.log_sigmoid(_hdot(a_low, wa_ref[...]) + ba_ref[...]) / GLA_TAU

    ri = lax.broadcasted_iota(I32, (c, c), 0)
    cj = lax.broadcasted_iota(I32, (c, c), 1)
    tril = ri >= cj
    tril_f = tril.astype(F32)
    ones_c = jnp.ones((c, LANES), F32)
    krow = lax.broadcasted_iota(I32, (kw, vw), 0) // GLA_DK
    vcol = lax.broadcasted_iota(I32, (kw, vw), 1) // GLA_DV
    blockdiag = krow == vcol
    klane = lax.broadcasted_iota(I32, (1, kw), 1) // GLA_DK
    vlane = lax.broadcasted_iota(I32, (1, vw), 1) // GLA_DV
    va = lax.broadcasted_iota(I32, (vw, vw), 0) // GLA_DV
    vb = lax.broadcasted_iota(I32, (vw, vw), 1) // GLA_DV
    head_avg = jnp.where(va == vb, 1.0 / GLA_DV, 0.0)

    def chunk(n, _):
        off = pl.multiple_of(n * c, c)
        rows = pl.ds(off, c)
        q = h_ref[rows, 0:kw] * (GLA_DK ** -0.5)
        k = h_ref[rows, kw:2 * kw]
        v = h_ref[rows, 2 * kw:2 * kw + vw]
        la = la_ref[rows, :]
        bcum = _hdot(tril_f, la)
        b_last = bcum[c - 1:c, :]
        q_t = q * jnp.exp(bcum)
        k_t = k * jnp.exp(-bcum)
        k_d = k * jnp.exp(b_last - bcum)
        state = st_ref[...]
        o = _hdot(q_t, state)
        for hd in range(GLA_HEADS):
            attn = _hdot_nt(jnp.where(klane == hd, q_t, 0.0), k_t)
            attn = jnp.where(tril, attn, 0.0)
            o = o + jnp.where(vlane == hd, _hdot(attn, v), 0.0)
        kv = jnp.where(blockdiag, _hdot_tn(k_d, v), 0.0)
        decay = jnp.exp(_hdot_tn(la, ones_c))
        st_ref[...] = jnp.concatenate([decay] * (vw // LANES), axis=1) * state + kv
        o_ref[rows, :] = o
        return 0

    lax.fori_loop(0, tb // c, chunk, 0)
    o = o_ref[...]
    o = o * lax.rsqrt(_hdot(o * o, head_avg) + EPS)
    r = h_ref[:, 2 * kw + vw:2 * kw + 2 * vw]
    y_ref[...] = (o * ng_ref[...] * (r * jax.nn.sigmoid(r))).astype(y_ref.dtype)


def _gla_mixer(hd, w_a2p, b_a, norm_g):
    seq, wid = hd.shape
    tb = min(seq, 512)
    kw = GLA_HEADS * GLA_DK
    vw = GLA_HEADS * GLA_DV
    full = lambda a: pl.BlockSpec(a.shape, lambda i: (0, 0))
    return pl.pallas_call(
        _gla_kernel,
        out_shape=jax.ShapeDtypeStruct((seq, vw), BF16),
        grid=(seq // tb,),
        in_specs=[pl.BlockSpec((tb, wid), lambda i: (i, 0)), full(w_a2p), full(b_a), full(norm_g)],
        out_specs=pl.BlockSpec((tb, vw), lambda i: (i, 0)),
        scratch_shapes=[pltpu.VMEM((kw, vw), F32), pltpu.VMEM((tb, kw), F32), pltpu.VMEM((tb, vw), F32)],
        compiler_params=_cparams(("arbitrary",)),
        name="gla_mixer",
    )(hd, w_a2p, b_a, norm_g)


def _merge_kernel(x_ref, g_ref, ys_ref, yn_ref, yg_ref, wgm_ref, ws_ref, wn_ref, wg_ref, wo_ref, o_ref):
    x = x_ref[...]
    d = x.shape[1]
    xn = _rms(x, g_ref[...]).astype(BF16)
    gm = jax.nn.sigmoid(jnp.dot(xn, wgm_ref[...], preferred_element_type=F32))
    mixed = (gm[:, 0:d] * jnp.dot(ys_ref[...], ws_ref[...], preferred_element_type=F32)
             + gm[:, d:2 * d] * jnp.dot(yn_ref[...], wn_ref[...], preferred_element_type=F32)
             + gm[:, 2 * d:3 * d] * jnp.dot(yg_ref[...], wg_ref[...], preferred_element_type=F32))
    o_ref[...] = x + _bdot(mixed, wo_ref[...])


def _merge(x, g, ys, yn, yg, wgm, ws, wn, wg, wo):
    seq, d = x.shape
    tm = min(seq, 512)
    row = lambda a: pl.BlockSpec((tm, a.shape[1]), lambda i: (i, 0))
    full = lambda a: pl.BlockSpec(a.shape, lambda i: (0, 0))
    return pl.pallas_call(
        _merge_kernel,
        out_shape=jax.ShapeDtypeStruct((seq, d), F32),
        grid=(seq // tm,),
        in_specs=[row(x), full(g), row(ys), row(yn), row(yg), full(wgm), full(ws), full(wn), full(wg),
                  full(wo)],
        out_specs=row(x),
        compiler_params=_cparams(("parallel",)),
        name="merge",
    )(x, g, ys, yn, yg, wgm, ws, wn, wg, wo)


def _ffn_kernel(x_ref, xp_ref, g_ref, wug_ref, wuv_ref, cwg_ref, cwv_ref, cbg_ref, cbv_ref, wd_ref,
                fg_ref, o_ref, xn_ref, *, final):
    i = pl.program_id(0)
    j = pl.program_id(1)
    tm = x_ref.shape[0]
    halo = xp_ref.shape[0]

    @pl.when(j == 0)
    def _():
        xn_ref[0:halo, :] = jnp.where(i == 0, 0.0, _rms(xp_ref[...], g_ref[...])).astype(BF16)
        xn_ref[halo:halo + tm, :] = _rms(x_ref[...], g_ref[...]).astype(BF16)

    xn = xn_ref[...]

    def conv(w_ref, cw_ref, cb_ref):
        h = jnp.dot(xn, w_ref[...], preferred_element_type=F32)
        hc = cb_ref[...]
        for t in range(CONV_WIDTH):
            sh = CONV_WIDTH - 1 - t
            hs = h if sh == 0 else pltpu.roll(h, sh, 0)
            hc = hc + cw_ref[t:t + 1, :] * hs[halo:halo + tm, :]
        return hc

    act = jax.nn.gelu(conv(wug_ref, cwg_ref, cbg_ref)) * conv(wuv_ref, cwv_ref, cbv_ref)
    part = _bdot(act, wd_ref[...])

    @pl.when(j == 0)
    def _():
        o_ref[...] = x_ref[...] + part

    @pl.when(j > 0)
    def _():
        o_ref[...] = o_ref[...] + part

    if final:
        @pl.when(j == pl.num_programs(1) - 1)
        def _():
            o_ref[...] = _rms(o_ref[...], fg_ref[...])


def _ffn(x, g, w_up, conv_w, conv_b, w_down, final_g, final):
    seq, d = x.shape
    dff = w_down.shape[0]
    tm = min(seq, 512)
    nj = 2
    tn = dff // nj
    halo = SUBLANES
    hb = tm // halo
    return pl.pallas_call(
        functools.partial(_ffn_kernel, final=final),
        out_shape=jax.ShapeDtypeStruct((seq, d), F32),
        grid=(seq // tm, nj),
        in_specs=[pl.BlockSpec((tm, d), lambda i, j: (i, 0)),
                  pl.BlockSpec((halo, d), lambda i, j: (jnp.maximum(i * hb - 1, 0), 0)),
                  pl.BlockSpec((1, d), lambda i, j: (0, 0)),
                  pl.BlockSpec((d, tn), lambda i, j: (0, j)),
                  pl.BlockSpec((d, tn), lambda i, j: (0, nj + j)),
                  pl.BlockSpec((CONV_WIDTH, tn), lambda i, j: (0, j)),
                  pl.BlockSpec((CONV_WIDTH, tn), lambda i, j: (0, nj + j)),
                  pl.BlockSpec((1, tn), lambda i, j: (0, j)),
                  pl.BlockSpec((1, tn), lambda i, j: (0, nj + j)),
                  pl.BlockSpec((tn, d), lambda i, j: (j, 0)),
                  pl.BlockSpec((1, d), lambda i, j: (0, 0))],
        out_specs=pl.BlockSpec((tm, d), lambda i, j: (i, 0)),
        scratch_shapes=[pltpu.VMEM((halo + tm, d), BF16)],
        compiler_params=_cparams(("parallel", "arbitrary")),
        name="conv_ffn",
    )(x, x, g, w_up, w_up, conv_w, conv_w, conv_b, conv_b, w_down, final_g)


def _split_w_in(w):
    d = w.shape[0]
    s5w = d // 4
    nsa_w = NSA_HEADS * HEAD_DIM
    kvw = NSA_KV_GROUPS * HEAD_DIM
    kw = GLA_HEADS * GLA_DK
    vw = GLA_HEADS * GLA_DV
    sizes = (s5w, nsa_w, kvw, kvw, kvw, kvw, kvw, kvw, 3 * NSA_HEADS, kw, kw, vw, 16, vw, 3 * d)
    assert sum(sizes) == w.shape[1]
    offs = [0]
    for sz in sizes:
        offs.append(offs[-1] + sz)
    col = lambda n: w[:, offs[n]:offs[n + 1]]
    (u, q, kc, vc, ks, vs, kwn, vwn, gn, gq, gk, gv, ga, gr, gm) = [col(n) for n in range(len(sizes))]
    padl = lambda a: jnp.pad(a, ((0, 0), (0, LANES - a.shape[1])))
    wa = u
    wb = jnp.concatenate([q, kc, ks, kwn], axis=1)
    wc = jnp.concatenate([vc, vs, vwn], axis=1)
    wd = jnp.concatenate([gq, gk, gv, gr, padl(ga), padl(gn)], axis=1)
    return [a.astype(BF16) for a in (wa, wb, wc, wd, gm)]


def kernel(x, positions, norm1_g, w_in, s5_lam_re, s5_lam_im, s5_log_dt, s5_b_re, s5_b_im, s5_c_re, s5_c_im, s5_d, s5_w_glu, s5_b_glu, nsa_pe_k, nsa_pe_v, nsa_ck_w1, nsa_ck_w2, nsa_cv_w1, nsa_cv_w2, gla_w_a2, gla_b_a, gla_norm_g, w_br_s5, w_br_nsa, w_br_gla, w_out, norm2_g, ffn_w_up, ffn_conv_w, ffn_conv_b, ffn_w_down, final_g):
    bsz, seq, d = x.shape
    depth = w_in.shape[0]
    assert bsz == 1 and seq % KV_TILE == 0 and d % LANES == 0
    g = NSA_KV_GROUPS
    dh = HEAD_DIM
    nqt = seq // Q_BLOCK
    nc = seq // CMP_STRIDE
    nsa_w = NSA_HEADS * dh
    kvw = g * dh
    kw = GLA_HEADS * GLA_DK
    vw = GLA_HEADS * GLA_DV
    row = lambda a: a.reshape(1, -1)

    xs = x.reshape(seq, d)
    tabs = _rope_tables(positions.reshape(seq))

    for l in range(depth):
        wa, wb, wc, wd, wgm = _split_w_in(w_in[l])
        g1 = row(norm1_g[l])
        u_s5, hb, hc, hd = _project(xs, g1, wa, wb, wc, wd, tabs, nsa_w)

        prep = _s5_prep(s5_lam_re[l], s5_lam_im[l], s5_log_dt[l], s5_b_re[l], s5_b_im[l],
                        s5_c_re[l], s5_c_im[l])
        y_s5 = _s5_mixer(u_s5, prep, row(s5_d[l]), s5_w_glu[l].astype(BF16), row(s5_b_glu[l]))

        q = hb[:, :nsa_w]
        qt = q.reshape(nqt, Q_BLOCK, g, NSA_HPG, dh).transpose(2, 0, 4, 3, 1).reshape(g, nqt, dh, NSA_HPG * Q_BLOCK)
        per_group = lambda a: a.reshape(seq, g, dh).transpose(1, 0, 2)
        to_rows = lambda a: per_group(a).reshape(g, nc, CMP_STRIDE * dh)
        k_cmp, k_slc, k_win = (hb[:, nsa_w + n * kvw:nsa_w + (n + 1) * kvw] for n in range(3))
        v_cmp, v_slc, v_win = (hc[:, n * kvw:(n + 1) * kvw] for n in range(3))
        r = jnp.stack([to_rows(k_cmp), to_rows(v_cmp)])
        pe = jnp.stack([nsa_pe_k[l].reshape(1, -1), nsa_pe_v[l].reshape(1, -1)])
        pe = jnp.broadcast_to(pe, (2, SUBLANES, pe.shape[-1]))
        w1 = jnp.stack([nsa_ck_w1[l], nsa_cv_w1[l]]).astype(BF16)
        w2 = jnp.stack([nsa_ck_w2[l], nsa_cv_w2[l]]).astype(BF16)
        cmp = _compress(r, w1, w2, pe)
        def values_t(v):
            n = v.shape[1]
            return jnp.concatenate([v.transpose(0, 2, 1), jnp.ones((g, 1, n), BF16),
                                    jnp.zeros((g, BF16_SUBLANES - 1, n), BF16)], axis=1)

        kc = cmp[0]
        vct = values_t(cmp[1])
        ks = per_group(k_slc)
        vst = values_t(per_group(v_slc))
        kwn = jnp.pad(per_group(k_win), ((0, 0), (WINDOW, 0), (0, 0)))
        vwt = values_t(jnp.pad(per_group(v_win), ((0, 0), (WINDOW, 0), (0, 0))))
        gates = hd[:, 2 * kw + 2 * vw + LANES:2 * kw + 2 * vw + LANES + 3 * NSA_HEADS]
        gates = gates.reshape(nqt, Q_BLOCK, g, 3 * NSA_HPG).transpose(0, 2, 3, 1)
        y_nsa = _nsa(qt, kc, vct, ks, vst, kwn, vwt, gates)

        w_a2p = jnp.pad(gla_w_a2[l], ((0, LANES - gla_w_a2.shape[1]), (0, 0)))
        y_gla = _gla_mixer(hd, w_a2p, row(gla_b_a[l]), row(gla_norm_g[l]))

        xs = _merge(xs, g1, y_s5, y_nsa, y_gla, wgm, w_br_s5[l].astype(BF16), w_br_nsa[l].astype(BF16),
                    w_br_gla[l].astype(BF16), w_out[l].astype(BF16))
        xs = _ffn(xs, row(norm2_g[l]), ffn_w_up[l].astype(BF16), ffn_conv_w[l], row(ffn_conv_b[l]),
                  ffn_w_down[l].astype(BF16), row(final_g), final=(l == depth - 1))
    return xs.reshape(bsz, seq, d)
```

```python
import functools
import math

import jax
import jax.numpy as jnp
from jax import lax
from jax.experimental import pallas as pl
from jax.experimental.pallas import tpu as pltpu

F32 = jnp.float32
BF16 = jnp.bfloat16
I32 = jnp.int32

EPS = 1e-6
NEG_INF = -1e30

S5_GROUP = 16
S5_STATE = 64
NSA_HEADS = 8
NSA_KV_GROUPS = 2
NSA_HPG = NSA_HEADS // NSA_KV_GROUPS
HEAD_DIM = 64
ROPE_DIM = 16
ROPE_THETA = 500000.0
CMP_BLOCK = 32
CMP_STRIDE = 16
SLC_BLOCK = 64
SLC_TOPK = 16
WINDOW = 512
Q_BLOCK = 128
FORCE_BONUS = 1.0e4
GLA_HEADS = 4
GLA_DK = 32
GLA_DV = 64
GLA_TAU = 16.0
GLA_CHUNK = 64
GLA_SUPER = 256
CONV_WIDTH = 3

LANES = 128
SUBLANES = 8
KV_TILE = 512
CMP_CHUNK = 256
BF16_SUBLANES = 16
VMEM_LIMIT = 56 * 1024 * 1024


def _cparams(sem):
    return pltpu.CompilerParams(dimension_semantics=sem, vmem_limit_bytes=VMEM_LIMIT)


def _bdot(a, b):
    return jnp.dot(a.astype(BF16), b.astype(BF16), preferred_element_type=F32)


def _bdot_nt(a, b):
    return lax.dot_general(a.astype(BF16), b.astype(BF16), (((1,), (1,)), ((), ())),
                           preferred_element_type=F32)


def _hdot(a, b):
    return jnp.dot(a, b, preferred_element_type=F32, precision=lax.Precision.HIGHEST)


def _hdot_nt(a, b):
    return lax.dot_general(a, b, (((1,), (1,)), ((), ())), preferred_element_type=F32,
                           precision=lax.Precision.HIGHEST)


def _hdot_tn(a, b):
    return lax.dot_general(a, b, (((0,), (0,)), ((), ())), preferred_element_type=F32,
                           precision=lax.Precision.HIGHEST)


def _rms(x, g):
    return x * lax.rsqrt(jnp.mean(x * x, axis=-1, keepdims=True) + EPS) * g


def _softmax_axis0(s, keep):
    s = jnp.where(keep, s, NEG_INF)
    m = jnp.max(s, axis=0, keepdims=True)
    p = jnp.where(keep, jnp.exp(s - m), 0.0)
    return p / jnp.maximum(jnp.sum(p, axis=0, keepdims=True), 1e-30)


def _rope_tab_kernel(pos_ref, invf_ref, c_ref, s1_ref, s2_ref):
    ang = pos_ref[...].astype(F32) * invf_ref[...]
    lane = lax.broadcasted_iota(I32, ang.shape, 1) % HEAD_DIM
    cosv = jnp.cos(ang)
    sinv = jnp.sin(ang)
    half = ROPE_DIM // 2
    c_ref[...] = jnp.where(lane < ROPE_DIM, cosv, 1.0)
    s1_ref[...] = jnp.where(lane < half, 0.0, jnp.where(lane < ROPE_DIM, sinv, 0.0))
    s2_ref[...] = jnp.where(lane < half, -sinv, 0.0)


def _rope_tables(positions):
    seq = positions.shape[0]
    tm = min(seq, 2048)
    inv_freq = ROPE_THETA ** (-jnp.arange(0, ROPE_DIM, 2, dtype=F32) / ROPE_DIM)
    lane = jnp.arange(LANES) % HEAD_DIM
    invf = jnp.where(lane < ROPE_DIM, inv_freq[lane % (ROPE_DIM // 2)], 0.0).reshape(1, LANES)
    tab = jax.ShapeDtypeStruct((seq, LANES), F32)
    return pl.pallas_call(
        _rope_tab_kernel,
        out_shape=(tab, tab, tab),
        grid=(seq // tm,),
        in_specs=[pl.BlockSpec((tm, 1), lambda i: (i, 0)),
                  pl.BlockSpec((1, LANES), lambda i: (0, 0))],
        out_specs=tuple(pl.BlockSpec((tm, LANES), lambda i: (i, 0)) for _ in range(3)),
        compiler_params=_cparams(("parallel",)),
        name="rope_tables",
    )(positions.reshape(seq, 1), invf)


def _proj_kernel(x_ref, g_ref, wa_ref, wb_ref, wc_ref, wd_ref, c_ref, s1_ref, s2_ref,
                 oa_ref, ob_ref, oc_ref, od_ref, *, q_width):
    xn = _rms(x_ref[...], g_ref[...]).astype(BF16)
    oa_ref[...] = jnp.dot(xn, wa_ref[...], preferred_element_type=F32)
    oc_ref[...] = jnp.dot(xn, wc_ref[...], preferred_element_type=F32).astype(BF16)
    od_ref[...] = jnp.dot(xn, wd_ref[...], preferred_element_type=F32)
    hb = jnp.dot(xn, wb_ref[...], preferred_element_type=F32)
    c = c_ref[...]
    s1 = s1_ref[...]
    s2 = s2_ref[...]
    half = ROPE_DIM // 2
    for j in range(hb.shape[1] // LANES):
        piece = hb[:, j * LANES:(j + 1) * LANES]
        rot = (piece * c + pltpu.roll(piece, half, 1) * s1
               + pltpu.roll(piece, LANES - half, 1) * s2)
        if j * LANES < q_width:
            rot = rot * (HEAD_DIM ** -0.5 * math.log2(math.e))
        ob_ref[:, j * LANES:(j + 1) * LANES] = rot.astype(BF16)


def _project(x, g, wa, wb, wc, wd, tabs, q_width):
    seq, d = x.shape
    tm = min(seq, 512)
    row = lambda w: pl.BlockSpec((tm, w), lambda i: (i, 0))
    full = lambda a: pl.BlockSpec(a.shape, lambda i: (0, 0))
    return pl.pallas_call(
        functools.partial(_proj_kernel, q_width=q_width),
        out_shape=(jax.ShapeDtypeStruct((seq, wa.shape[1]), F32),
                   jax.ShapeDtypeStruct((seq, wb.shape[1]), BF16),
                   jax.ShapeDtypeStruct((seq, wc.shape[1]), BF16),
                   jax.ShapeDtypeStruct((seq, wd.shape[1]), F32)),
        grid=(seq // tm,),
        in_specs=[row(d), full(g), full(wa), full(wb), full(wc), full(wd),
                  row(LANES), row(LANES), row(LANES)],
        out_specs=(row(wa.shape[1]), row(wb.shape[1]), row(wc.shape[1]), row(wd.shape[1])),
        compiler_params=_cparams(("parallel",)),
        name="in_proj",
    )(x, g, wa, wb, wc, wd, *tabs)


def _s5_prep_kernel(lr_ref, li_ref, ldt_ref, btr_ref, bti_ref, ctr_ref, cti_ref,
                    bre_ref, bim_ref, cre_ref, cim_ref, tab_ref):
    lr = lr_ref[...]
    li = li_ref[...]
    dt = jnp.exp(ldt_ref[...])
    mag = jnp.exp(lr * dt)
    ar = mag * jnp.cos(li * dt)
    ai = mag * jnp.sin(li * dt)
    nr = ar - 1.0
    ni = ai
    den = lr * lr + li * li
    f_re = (nr * lr + ni * li) / den
    f_im = (ni * lr - nr * li) / den
    btr = btr_ref[...]
    bti = bti_ref[...]
    rows = lax.broadcasted_iota(I32, btr.shape, 0) // S5_GROUP
    cols = lax.broadcasted_iota(I32, btr.shape, 1) // S5_STATE
    diag = rows == cols
    bre_ref[...] = jnp.where(diag, f_re * btr - f_im * bti, 0.0).astype(BF16)
    bim_ref[...] = jnp.where(diag, f_re * bti + f_im * btr, 0.0).astype(BF16)
    cre_ref[...] = jnp.where(diag, ctr_ref[...], 0.0).astype(BF16)
    cim_ref[...] = jnp.where(diag, cti_ref[...], 0.0).astype(BF16)
    pr = [ar]
    pi = [ai]
    for _ in range(SUBLANES - 1):
        pr_n = pr[-1] * ar - pi[-1] * ai
        pi_n = pr[-1] * ai + pi[-1] * ar
        pr.append(pr_n)
        pi.append(pi_n)
    rid = lax.broadcasted_iota(I32, (SUBLANES, lr.shape[1]), 0)
    zero = jnp.zeros((SUBLANES, lr.shape[1]), F32)
    p_re = zero
    p_im = zero
    for k in range(SUBLANES):
        p_re = jnp.where(rid == k, pr[k], p_re)
        p_im = jnp.where(rid == k, pi[k], p_im)
    for n, k in enumerate((1, 2, 4)):
        tab_ref[2 * n] = jnp.where(rid >= k, pr[k - 1], 0.0)
        tab_ref[2 * n + 1] = jnp.where(rid >= k, pi[k - 1], 0.0)
    tab_ref[6] = p_re
    tab_ref[7] = p_im


def _s5_prep(lam_re, lam_im, log_dt, b_re, b_im, c_re, c_im):
    g, p = lam_re.shape
    h = b_re.shape[-1]
    gp = g * p
    rowv = lambda a: a.reshape(1, gp)
    tile_t = lambda a: jnp.tile(a.reshape(h, gp), (g, 1))
    btr = tile_t(jnp.transpose(b_re, (2, 0, 1)))
    bti = tile_t(jnp.transpose(b_im, (2, 0, 1)))
    ctr = tile_t(jnp.transpose(c_re, (1, 0, 2)))
    cti = tile_t(jnp.transpose(c_im, (1, 0, 2)))
    mat = jax.ShapeDtypeStruct((g * h, gp), BF16)
    return pl.pallas_call(
        _s5_prep_kernel,
        out_shape=(mat, mat, mat, mat, jax.ShapeDtypeStruct((8, SUBLANES, gp), F32)),
        name="s5_prep",
    )(rowv(lam_re), rowv(lam_im), rowv(jnp.repeat(log_dt, p)), btr, bti, ctr, cti)


def _s5_kernel(u_ref, bre_ref, bim_ref, cre_ref, cim_ref, tab_ref, d_ref, wg_ref, bg_ref,
               y_ref, xr_ref, xi_ref, car_ref):
    @pl.when(pl.program_id(0) == 0)
    def _():
        car_ref[...] = jnp.zeros_like(car_ref)

    u = u_ref[...]
    ub = u.astype(BF16)
    xr_ref[...] = jnp.dot(ub, bre_ref[...], preferred_element_type=F32)
    xi_ref[...] = jnp.dot(ub, bim_ref[...], preferred_element_type=F32)
    tb = u.shape[0]

    def slab(r, carry):
        cr, ci = carry
        off = pl.multiple_of(r * SUBLANES, SUBLANES)
        xr = xr_ref[pl.ds(off, SUBLANES), :]
        xi = xi_ref[pl.ds(off, SUBLANES), :]
        for n, k in enumerate((1, 2, 4)):
            tr = tab_ref[2 * n]
            ti = tab_ref[2 * n + 1]
            sr = pltpu.roll(xr, k, 0)
            si = pltpu.roll(xi, k, 0)
            xr, xi = xr + tr * sr - ti * si, xi + tr * si + ti * sr
        pr = tab_ref[6]
        pi = tab_ref[7]
        xr, xi = xr + pr * cr - pi * ci, xi + pr * ci + pi * cr
        xr_ref[pl.ds(off, SUBLANES), :] = xr
        xi_ref[pl.ds(off, SUBLANES), :] = xi
        return xr[SUBLANES - 1:SUBLANES, :], xi[SUBLANES - 1:SUBLANES, :]

    cr, ci = lax.fori_loop(0, tb // SUBLANES, slab, (car_ref[0:1, :], car_ref[1:2, :]))
    car_ref[0:1, :] = cr
    car_ref[1:2, :] = ci
    y = _bdot_nt(xr_ref[...], cre_ref[...]) - _bdot_nt(xi_ref[...], cim_ref[...])
    y = y + d_ref[...] * u
    y = jax.nn.gelu(y)
    y = y * jax.nn.sigmoid(_bdot(y, wg_ref[...]) + bg_ref[...])
    y_ref[...] = y.astype(y_ref.dtype)


def _s5_mixer(u, prep, d_skip, w_glu, b_glu):
    seq, w = u.shape
    bre, bim, cre, cim, tab = prep
    gp = bre.shape[1]
    tb = min(seq, 512)
    full = lambda a: pl.BlockSpec(a.shape, lambda i: (0,) * a.ndim)
    return pl.pallas_call(
        _s5_kernel,
        out_shape=jax.ShapeDtypeStruct((seq, w), BF16),
        grid=(seq // tb,),
        in_specs=[pl.BlockSpec((tb, w), lambda i: (i, 0)), full(bre), full(bim), full(cre),
                  full(cim), full(tab), full(d_skip), full(w_glu), full(b_glu)],
        out_specs=pl.BlockSpec((tb, w), lambda i: (i, 0)),
        scratch_shapes=[pltpu.VMEM((tb, gp), F32), pltpu.VMEM((tb, gp), F32),
                        pltpu.VMEM((SUBLANES, gp), F32)],
        compiler_params=_cparams(("arbitrary",)),
        name="s5_mixer",
    )(u, bre, bim, cre, cim, tab, d_skip, w_glu, b_glu)


def _compress_kernel(r_ref, w1_ref, w2_ref, pe_ref, o_ref):
    r = r_ref[0, 0]
    half = r.shape[1]
    w1 = w1_ref[0]
    a = jnp.dot(r, w1[:half], preferred_element_type=F32)
    b = jnp.dot(r, w1[half:], preferred_element_type=F32)
    pew = _bdot(pe_ref[0], w1)
    nc = r.shape[0]
    hid = a + pltpu.roll(b, nc - 1, 0) + pew[0:1, :]
    o_ref[0, 0] = _bdot(jax.nn.gelu(hid), w2_ref[0]).astype(o_ref.dtype)


def _compress(r, w1, w2, pe):
    two, g, nc, wid = r.shape
    dh = w2.shape[-1]
    return pl.pallas_call(
        _compress_kernel,
        out_shape=jax.ShapeDtypeStruct((two, g, nc, dh), BF16),
        grid=(two, g),
        in_specs=[pl.BlockSpec((1, 1, nc, wid), lambda a, b: (a, b, 0, 0)),
                  pl.BlockSpec((1,) + w1.shape[1:], lambda a, b: (a, 0, 0)),
                  pl.BlockSpec((1,) + w2.shape[1:], lambda a, b: (a, 0, 0)),
                  pl.BlockSpec((1,) + pe.shape[1:], lambda a, b: (a, 0, 0))],
        out_specs=pl.BlockSpec((1, 1, nc, dh), lambda a, b: (a, b, 0, 0)),
        compiler_params=_cparams(("parallel", "parallel")),
        name="compress",
    )(r, w1, w2, pe)


def _nsa_kernel(qt_ref, kc_ref, vct_ref, ks_ref, vst_ref, kw_ref, vwt_ref, g_ref,
                out_ref, sc_ref, imp_ref, selb_ref, sa_ref, sb_ref, *, nc, ns):
    qi = pl.program_id(1)
    qt = qt_ref[0, 0]
    dh, width = qt.shape
    vrows = vst_ref.shape[1]
    lane = lax.broadcasted_iota(I32, (1, width), 1)
    t_row = qi * Q_BLOCK + (lane % Q_BLOCK)
    tq = qi * Q_BLOCK + lax.broadcasted_iota(I32, (1, Q_BLOCK), 1)
    all_heads = lambda a: jnp.concatenate([a] * NSA_HPG, axis=1)

    pad = SUBLANES
    ch = min(CMP_CHUNK, nc)
    n_ch = (((qi + 1) * Q_BLOCK - CMP_BLOCK) // CMP_STRIDE) // ch + 1

    def cmp_scores(c, m):
        off = pl.multiple_of(c * ch, ch)
        s = jnp.dot(kc_ref[0, pl.ds(off, ch), :], qt, preferred_element_type=F32)
        ci = off + lax.broadcasted_iota(I32, (ch, 1), 0)
        s = jnp.where(ci * CMP_STRIDE + (CMP_BLOCK - 1) <= t_row, s, NEG_INF)
        sc_ref[pl.ds(off, ch), :] = s
        return jnp.maximum(m, jnp.max(s, axis=0, keepdims=True))

    m = lax.fori_loop(0, n_ch, cmp_scores, jnp.full((1, width), NEG_INF, F32))
    m = jnp.maximum(m, 0.1 * NEG_INF)

    def cmp_probs(c, acc):
        off = pl.multiple_of(c * ch, ch)
        e = jnp.exp2(sc_ref[pl.ds(off, ch), :] - m)
        sc_ref[pl.ds(off, ch), :] = e
        return acc + jnp.dot(vct_ref[0, :, pl.ds(off, ch)], e.astype(BF16), preferred_element_type=F32)

    acc = lax.fori_loop(0, n_ch, cmp_probs, jnp.zeros((vrows, width), F32))
    inv_l = 1.0 / jnp.maximum(acc[dh:dh + 1, :], 1e-30)
    o_cmp = acc[0:dh, :] * inv_l
    imp_ref[...] = jnp.zeros_like(imp_ref)

    def cmp_importance(c, _):
        off = pl.multiple_of(c * ch, ch)
        p = sc_ref[pl.ds(off, ch), :] * inv_l
        imp = p[:, 0:Q_BLOCK]
        for h in range(1, NSA_HPG):
            imp = imp + p[:, h * Q_BLOCK:(h + 1) * Q_BLOCK]
        imp_ref[pl.ds(pl.multiple_of(pad + off, pad), ch), :] = imp
        return 0

    lax.fori_loop(0, n_ch, cmp_importance, 0)

    sa_ref[...] = jnp.dot(ks_ref[0, 0:KV_TILE, :], qt, preferred_element_type=F32)
    span_w = WINDOW + Q_BLOCK
    woff = pl.multiple_of(qi * Q_BLOCK, Q_BLOCK)
    s = jnp.dot(kw_ref[0, pl.ds(woff, span_w), :], qt, preferred_element_type=F32)
    r = lax.broadcasted_iota(I32, (span_w, 1), 0)
    dpos = lax.broadcasted_iota(I32, (1, Q_BLOCK), 1) + WINDOW - r
    ok = (dpos >= 0) & (dpos < WINDOW) & (qi * Q_BLOCK - WINDOW + r >= 0)
    s = s + all_heads(jnp.where(ok, 0.0, NEG_INF))
    p = jnp.exp2(s - jnp.max(s, axis=0, keepdims=True)).astype(BF16)
    acc = jnp.dot(vwt_ref[0, :, pl.ds(woff, span_w)], p, preferred_element_type=F32)
    o_win = acc[0:dh, :] / jnp.maximum(acc[dh:dh + 1, :], 1e-30)

    ratio = SLC_BLOCK // CMP_STRIDE
    span = ratio + CMP_BLOCK // CMP_STRIDE - 1
    lead = CMP_BLOCK // CMP_STRIDE - 1
    imp_slc = imp_ref[pl.ds(pad - lead, ns, stride=ratio), :]
    for sft in range(1, span):
        imp_slc = imp_slc + imp_ref[pl.ds(pad - lead + sft, ns, stride=ratio), :]

    jb = lax.broadcasted_iota(I32, (ns, 1), 0)
    cur = tq // SLC_BLOCK
    forced = (jb == 0) | (jb == cur) | (jb == cur - 1)
    score = jnp.where(jb * SLC_BLOCK <= tq, imp_slc + forced.astype(F32) * FORCE_BONUS, NEG_INF)
    jf = jb.astype(F32)
    selb = jnp.full((ns, Q_BLOCK), NEG_INF, F32)
    for _ in range(min(SLC_TOPK, ns)):
        mx = jnp.max(score, axis=0, keepdims=True)
        first = jnp.min(jnp.where(score == mx, jf, float(ns)), axis=0, keepdims=True)
        hit = jf == first
        selb = jnp.where(hit, 0.0, selb)
        score = jnp.where(hit, -jnp.inf, score)
    selb_ref[...] = selb

    blocks = KV_TILE // SLC_BLOCK

    def causal(kt, visible):
        kpos = kt * KV_TILE + lax.broadcasted_iota(I32, (KV_TILE, 1), 0)
        return jnp.where(kpos <= tq, visible, NEG_INF)

    def tile_bias(kt, is_causal):
        bias_t = selb_ref[pl.ds(pl.multiple_of(kt * blocks, blocks), blocks), :]
        bias = jnp.concatenate(
            [jnp.broadcast_to(bias_t[b:b + 1, :], (SLC_BLOCK, Q_BLOCK)) for b in range(blocks)], axis=0)
        return all_heads(causal(kt, bias) if is_causal else bias)

    def scores(kt, dst_ref, is_causal):
        off = pl.multiple_of(kt * KV_TILE, KV_TILE)
        dst_ref[...] = jnp.dot(ks_ref[0, pl.ds(off, KV_TILE), :], qt,
                               preferred_element_type=F32) + tile_bias(kt, is_causal)

    def consume(kt, src_ref, m, acc, extra_bias=None):
        off = pl.multiple_of(kt * KV_TILE, KV_TILE)
        s = src_ref[...]
        if extra_bias is not None:
            s = s + extra_bias
        m_new = jnp.maximum(m, jnp.max(s, axis=0, keepdims=True))
        p = jnp.exp2(s - m_new).astype(BF16)
        acc = jnp.exp2(m - m_new) * acc + jnp.dot(vst_ref[0, :, pl.ds(off, KV_TILE)], p,
                                                  preferred_element_type=F32)
        return m_new, acc

    n_before = (qi * Q_BLOCK) // KV_TILE
    sa_ref[...] = sa_ref[...] + tile_bias(0, False)

    def tile_pair(i, carry):
        m, acc = carry
        scores(2 * i + 1, sb_ref, False)
        m, acc = consume(2 * i, sa_ref, m, acc)
        scores(2 * i + 2, sa_ref, False)
        return consume(2 * i + 1, sb_ref, m, acc)

    init = (jnp.full((1, width), NEG_INF, F32), jnp.zeros((vrows, width), F32))
    m, acc = lax.fori_loop(0, n_before // 2, tile_pair, init)
    k0 = 2 * (n_before // 2)
    scores(k0 + 1, sb_ref, True)
    m, acc = consume(k0, sa_ref, m, acc, all_heads(causal(k0, 0.0)))
    _, acc = consume(k0 + 1, sb_ref, m, acc)
    o_slc = acc[0:dh, :] / jnp.maximum(acc[dh:dh + 1, :], 1e-30)

    gate = jax.nn.sigmoid(g_ref[0, 0])
    outs = []
    for h in range(NSA_HPG):
        hs = slice(h * Q_BLOCK, (h + 1) * Q_BLOCK)
        y = (gate[3 * h:3 * h + 1, :] * o_cmp[:, hs] + gate[3 * h + 1:3 * h + 2, :] * o_slc[:, hs]
             + gate[3 * h + 2:3 * h + 3, :] * o_win[:, hs])
        outs.append(y.T)
    out_ref[...] = jnp.concatenate(outs, axis=1).astype(out_ref.dtype)


def _nsa(qt, kc, vct, ks, vst, kw, vwt, gates):
    g, nqt, dh, width = qt.shape
    seq = ks.shape[1]
    nc = kc.shape[1]
    ns = seq // SLC_BLOCK
    per_g = lambda a: pl.BlockSpec((1,) + a.shape[1:], lambda gi, qi: (gi,) + (0,) * (a.ndim - 1))
    return pl.pallas_call(
        functools.partial(_nsa_kernel, nc=nc, ns=ns),
        out_shape=jax.ShapeDtypeStruct((seq, g * NSA_HPG * dh), BF16),
        grid=(g, nqt),
        in_specs=[pl.BlockSpec((1, 1, dh, width), lambda gi, qi: (gi, qi, 0, 0)),
                  per_g(kc), per_g(vct), per_g(ks), per_g(vst), per_g(kw), per_g(vwt),
                  pl.BlockSpec((1, 1) + gates.shape[2:], lambda gi, qi: (qi, gi, 0, 0))],
        out_specs=pl.BlockSpec((Q_BLOCK, NSA_HPG * dh), lambda gi, qi: (qi, gi)),
        scratch_shapes=[pltpu.VMEM((nc, width), F32),
                        pltpu.VMEM((nc + 2 * SUBLANES, Q_BLOCK), F32),
                        pltpu.VMEM((ns, Q_BLOCK), F32),
                        pltpu.VMEM((KV_TILE, width), F32),
                        pltpu.VMEM((KV_TILE, width), F32)],
        compiler_params=_cparams(("parallel", "arbitrary")),
        name="sparse_attention",
    )(qt, kc, vct, ks, vst, kw, vwt, gates)


def _split_bf16(x):
    hi = x.astype(BF16)
    return hi, (x - hi.astype(F32)).astype(BF16)


def _gla_kernel(h_ref, wa_ref, ba_ref, ng_ref, y_ref, st_ref):
    @pl.when(pl.program_id(0) == 0)
    def _():
        st_ref[...] = jnp.zeros_like(st_ref)

    kw = GLA_HEADS * GLA_DK
    vw = GLA_HEADS * GLA_DV
    c = GLA_CHUNK
    sb = min(GLA_SUPER, h_ref.shape[0])
    ri = lax.broadcasted_iota(I32, (sb, sb), 0)
    cj = lax.broadcasted_iota(I32, (sb, sb), 1)
    same_chunk = (ri // c) == (cj // c)
    tril = same_chunk & (ri >= cj)
    tril_m = jnp.where(tril, 1.0, 0.0).astype(BF16)
    chunk_m = jnp.where(same_chunk, 1.0, 0.0).astype(BF16)
    ones_c = jnp.ones((c, LANES), BF16)
    krow = lax.broadcasted_iota(I32, (kw, vw), 0) // GLA_DK
    vcol = lax.broadcasted_iota(I32, (kw, vw), 1) // GLA_DV
    blockdiag = krow == vcol
    klane = lax.broadcasted_iota(I32, (1, kw), 1) // GLA_DK
    vlane = lax.broadcasted_iota(I32, (1, vw), 1) // GLA_DV
    va = lax.broadcasted_iota(I32, (vw, vw), 0) // GLA_DV
    vb = lax.broadcasted_iota(I32, (vw, vw), 1) // GLA_DV
    head_avg = jnp.where(va == vb, 1.0 / GLA_DV, 0.0).astype(BF16)
    tn = (((0,), (0,)), ((), ()))

    state = st_ref[...]
    for blk in range(h_ref.shape[0] // sb):
        rows = slice(blk * sb, (blk + 1) * sb)
        q = h_ref[rows, 0:kw] * (GLA_DK ** -0.5)
        k = h_ref[rows, kw:2 * kw]
        v = h_ref[rows, 2 * kw:2 * kw + vw].astype(BF16)
        r = h_ref[rows, 2 * kw + vw:2 * kw + 2 * vw]
        a_low = h_ref[rows, 2 * kw + 2 * vw:2 * kw + 2 * vw + LANES]
        la = jax.nn.log_sigmoid(_hdot(a_low, wa_ref[...]) + ba_ref[...]) / GLA_TAU
        la_hi, la_lo = _split_bf16(la)
        sum_la = lambda mat: (jnp.dot(mat, la_hi, preferred_element_type=F32)
                              + jnp.dot(mat, la_lo, preferred_element_type=F32))
        bcum = sum_la(tril_m)
        b_last = sum_la(chunk_m)
        q_t = (q * jnp.exp(bcum)).astype(BF16)
        k_t = (k * jnp.exp(-bcum)).astype(BF16)
        k_d = (k * jnp.exp(b_last - bcum)).astype(BF16)
        o = jnp.zeros((sb, vw), F32)
        for hd in range(GLA_HEADS):
            attn = _bdot_nt(jnp.where(klane == hd, q_t, jnp.zeros_like(q_t)), k_t)
            attn = jnp.where(tril, attn, 0.0).astype(BF16)
            o = o + jnp.where(vlane == hd, jnp.dot(attn, v, preferred_element_type=F32), 0.0)
        inter = []
        for n in range(sb // c):
            cs = slice(n * c, (n + 1) * c)
            inter.append(jnp.dot(q_t[cs], state.astype(BF16), preferred_element_type=F32))
            kv = jnp.where(blockdiag, lax.dot_general(k_d[cs], v[cs], tn, preferred_element_type=F32), 0.0)
            tot = (lax.dot_general(la_hi[cs], ones_c, tn, preferred_element_type=F32)
                   + lax.dot_general(la_lo[cs], ones_c, tn, preferred_element_type=F32))
            decay = jnp.exp(tot)
            state = jnp.concatenate([decay] * (vw // LANES), axis=1) * state + kv
        o = o + jnp.concatenate(inter, axis=0)
        sq_hi, sq_lo = _split_bf16(o * o)
        ms = (jnp.dot(sq_hi, head_avg, preferred_element_type=F32)
              + jnp.dot(sq_lo, head_avg, preferred_element_type=F32))
        o = o * lax.rsqrt(ms + EPS)
        y_ref[rows, :] = (o * ng_ref[...] * (r * jax.nn.sigmoid(r))).astype(y_ref.dtype)
    st_ref[...] = state


def _gla_mixer(hd, w_a2p, b_a, norm_g):
    seq, wid = hd.shape
    tb = min(seq, 512)
    kw = GLA_HEADS * GLA_DK
    vw = GLA_HEADS * GLA_DV
    full = lambda a: pl.BlockSpec(a.shape, lambda i: (0, 0))
    return pl.pallas_call(
        _gla_kernel,
        out_shape=jax.ShapeDtypeStruct((seq, vw), BF16),
        grid=(seq // tb,),
        in_specs=[pl.BlockSpec((tb, wid), lambda i: (i, 0)), full(w_a2p), full(b_a), full(norm_g)],
        out_specs=pl.BlockSpec((tb, vw), lambda i: (i, 0)),
        scratch_shapes=[pltpu.VMEM((kw, vw), F32)],
        compiler_params=_cparams(("arbitrary",)),
        name="gla_mixer",
    )(hd, w_a2p, b_a, norm_g)


def _merge_kernel(x_ref, g_ref, ys_ref, yn_ref, yg_ref, wgm_ref, ws_ref, wn_ref, wg_ref, wo_ref, o_ref):
    x = x_ref[...]
    d = x.shape[1]
    xn = _rms(x, g_ref[...]).astype(BF16)
    gm = jax.nn.sigmoid(jnp.dot(xn, wgm_ref[...], preferred_element_type=F32))
    mixed = (gm[:, 0:d] * jnp.dot(ys_ref[...], ws_ref[...], preferred_element_type=F32)
             + gm[:, d:2 * d] * jnp.dot(yn_ref[...], wn_ref[...], preferred_element_type=F32)
             + gm[:, 2 * d:3 * d] * jnp.dot(yg_ref[...], wg_ref[...], preferred_element_type=F32))
    o_ref[...] = x + _bdot(mixed, wo_ref[...])


def _merge(x, g, ys, yn, yg, wgm, ws, wn, wg, wo):
    seq, d = x.shape
    tm = min(seq, 512)
    row = lambda a: pl.BlockSpec((tm, a.shape[1]), lambda i: (i, 0))
    full = lambda a: pl.BlockSpec(a.shape, lambda i: (0, 0))
    return pl.pallas_call(
        _merge_kernel,
        out_shape=jax.ShapeDtypeStruct((seq, d), F32),
        grid=(seq // tm,),
        in_specs=[row(x), full(g), row(ys), row(yn), row(yg), full(wgm), full(ws), full(wn), full(wg),
                  full(wo)],
        out_specs=row(x),
        compiler_params=_cparams(("parallel",)),
        name="merge",
    )(x, g, ys, yn, yg, wgm, ws, wn, wg, wo)


def _ffn_kernel(x_ref, xp_ref, g_ref, wug_ref, wuv_ref, cwg_ref, cwv_ref, cbg_ref, cbv_ref, wd_ref,
                fg_ref, o_ref, xn_ref, *, final):
    i = pl.program_id(0)
    j = pl.program_id(1)
    tm = x_ref.shape[0]
    halo = xp_ref.shape[0]

    @pl.when(j == 0)
    def _():
        xn_ref[0:halo, :] = jnp.where(i == 0, 0.0, _rms(xp_ref[...], g_ref[...])).astype(BF16)
        xn_ref[halo:halo + tm, :] = _rms(x_ref[...], g_ref[...]).astype(BF16)

    xn = xn_ref[...]

    def conv(w_ref, cw_ref, cb_ref):
        h = jnp.dot(xn, w_ref[...], preferred_element_type=F32)
        hc = cb_ref[...]
        for t in range(CONV_WIDTH):
            sh = CONV_WIDTH - 1 - t
            hs = h if sh == 0 else pltpu.roll(h, sh, 0)
            hc = hc + cw_ref[t:t + 1, :] * hs[halo:halo + tm, :]
        return hc

    act = jax.nn.gelu(conv(wug_ref, cwg_ref, cbg_ref)) * conv(wuv_ref, cwv_ref, cbv_ref)
    part = _bdot(act, wd_ref[...])

    @pl.when(j == 0)
    def _():
        o_ref[...] = x_ref[...] + part

    @pl.when(j > 0)
    def _():
        o_ref[...] = o_ref[...] + part

    if final:
        @pl.when(j == pl.num_programs(1) - 1)
        def _():
            o_ref[...] = _rms(o_ref[...], fg_ref[...])


def _ffn(x, g, w_up, conv_w, conv_b, w_down, final_g, final):
    seq, d = x.shape
    dff = w_down.shape[0]
    tm = min(seq, 512)
    nj = 2
    tn = dff // nj
    halo = SUBLANES
    hb = tm // halo
    return pl.pallas_call(
        functools.partial(_ffn_kernel, final=final),
        out_shape=jax.ShapeDtypeStruct((seq, d), F32),
        grid=(seq // tm, nj),
        in_specs=[pl.BlockSpec((tm, d), lambda i, j: (i, 0)),
                  pl.BlockSpec((halo, d), lambda i, j: (jnp.maximum(i * hb - 1, 0), 0)),
                  pl.BlockSpec((1, d), lambda i, j: (0, 0)),
                  pl.BlockSpec((d, tn), lambda i, j: (0, j)),
                  pl.BlockSpec((d, tn), lambda i, j: (0, nj + j)),
                  pl.BlockSpec((CONV_WIDTH, tn), lambda i, j: (0, j)),
                  pl.BlockSpec((CONV_WIDTH, tn), lambda i, j: (0, nj + j)),
                  pl.BlockSpec((1, tn), lambda i, j: (0, j)),
                  pl.BlockSpec((1, tn), lambda i, j: (0, nj + j)),
                  pl.BlockSpec((tn, d), lambda i, j: (j, 0)),
                  pl.BlockSpec((1, d), lambda i, j: (0, 0))],
        out_specs=pl.BlockSpec((tm, d), lambda i, j: (i, 0)),
        scratch_shapes=[pltpu.VMEM((halo + tm, d), BF16)],
        compiler_params=_cparams(("parallel", "arbitrary")),
        name="conv_ffn",
    )(x, x, g, w_up, w_up, conv_w, conv_w, conv_b, conv_b, w_down, final_g)


def _split_w_in(w):
    d = w.shape[0]
    s5w = d // 4
    nsa_w = NSA_HEADS * HEAD_DIM
    kvw = NSA_KV_GROUPS * HEAD_DIM
    kw = GLA_HEADS * GLA_DK
    vw = GLA_HEADS * GLA_DV
    sizes = (s5w, nsa_w, kvw, kvw, kvw, kvw, kvw, kvw, 3 * NSA_HEADS, kw, kw, vw, 16, vw, 3 * d)
    assert sum(sizes) == w.shape[1]
    offs = [0]
    for sz in sizes:
        offs.append(offs[-1] + sz)
    col = lambda n: w[:, offs[n]:offs[n + 1]]
    (u, q, kc, vc, ks, vs, kwn, vwn, gn, gq, gk, gv, ga, gr, gm) = [col(n) for n in range(len(sizes))]
    padl = lambda a: jnp.pad(a, ((0, 0), (0, LANES - a.shape[1])))
    wa = u
    wb = jnp.concatenate([q, kc, ks, kwn], axis=1)
    wc = jnp.concatenate([vc, vs, vwn], axis=1)
    wd = jnp.concatenate([gq, gk, gv, gr, padl(ga), padl(gn)], axis=1)
    return [a.astype(BF16) for a in (wa, wb, wc, wd, gm)]


def kernel(x, positions, norm1_g, w_in, s5_lam_re, s5_lam_im, s5_log_dt, s5_b_re, s5_b_im, s5_c_re, s5_c_im, s5_d, s5_w_glu, s5_b_glu, nsa_pe_k, nsa_pe_v, nsa_ck_w1, nsa_ck_w2, nsa_cv_w1, nsa_cv_w2, gla_w_a2, gla_b_a, gla_norm_g, w_br_s5, w_br_nsa, w_br_gla, w_out, norm2_g, ffn_w_up, ffn_conv_w, ffn_conv_b, ffn_w_down, final_g):
    bsz, seq, d = x.shape
    depth = w_in.shape[0]
    assert bsz == 1 and seq % KV_TILE == 0 and d % LANES == 0
    g = NSA_KV_GROUPS
    dh = HEAD_DIM
    nqt = seq // Q_BLOCK
    nc = seq // CMP_STRIDE
    nsa_w = NSA_HEADS * dh
    kvw = g * dh
    kw = GLA_HEADS * GLA_DK
    vw = GLA_HEADS * GLA_DV
    row = lambda a: a.reshape(1, -1)

    xs = x.reshape(seq, d)
    tabs = _rope_tables(positions.reshape(seq))

    for l in range(depth):
        wa, wb, wc, wd, wgm = _split_w_in(w_in[l])
        g1 = row(norm1_g[l])
        u_s5, hb, hc, hd = _project(xs, g1, wa, wb, wc, wd, tabs, nsa_w)

        prep = _s5_prep(s5_lam_re[l], s5_lam_im[l], s5_log_dt[l], s5_b_re[l], s5_b_im[l],
                        s5_c_re[l], s5_c_im[l])
        y_s5 = _s5_mixer(u_s5, prep, row(s5_d[l]), s5_w_glu[l].astype(BF16), row(s5_b_glu[l]))

        q = hb[:, :nsa_w]
        qt = q.reshape(nqt, Q_BLOCK, g, NSA_HPG, dh).transpose(2, 0, 4, 3, 1).reshape(g, nqt, dh, NSA_HPG * Q_BLOCK)
        per_group = lambda a: a.reshape(seq, g, dh).transpose(1, 0, 2)
        to_rows = lambda a: per_group(a).reshape(g, nc, CMP_STRIDE * dh)
        k_cmp, k_slc, k_win = (hb[:, nsa_w + n * kvw:nsa_w + (n + 1) * kvw] for n in range(3))
        v_cmp, v_slc, v_win = (hc[:, n * kvw:(n + 1) * kvw] for n in range(3))
        r = jnp.stack([to_rows(k_cmp), to_rows(v_cmp)])
        pe = jnp.stack([nsa_pe_k[l].reshape(1, -1), nsa_pe_v[l].reshape(1, -1)])
        pe = jnp.broadcast_to(pe, (2, SUBLANES, pe.shape[-1]))
        w1 = jnp.stack([nsa_ck_w1[l], nsa_cv_w1[l]]).astype(BF16)
        w2 = jnp.stack([nsa_ck_w2[l], nsa_cv_w2[l]]).astype(BF16)
        cmp = _compress(r, w1, w2, pe)
        def values_t(v):
            n = v.shape[1]
            return jnp.concatenate([v.transpose(0, 2, 1), jnp.ones((g, 1, n), BF16),
                                    jnp.zeros((g, BF16_SUBLANES - 1, n), BF16)], axis=1)

        kc = cmp[0]
        vct = values_t(cmp[1])
        ks = per_group(k_slc)
        vst = values_t(per_group(v_slc))
        kwn = jnp.pad(per_group(k_win), ((0, 0), (WINDOW, 0), (0, 0)))
        vwt = values_t(jnp.pad(per_group(v_win), ((0, 0), (WINDOW, 0), (0, 0))))
        gates = hd[:, 2 * kw + 2 * vw + LANES:2 * kw + 2 * vw + LANES + 3 * NSA_HEADS]
        gates = gates.reshape(nqt, Q_BLOCK, g, 3 * NSA_HPG).transpose(0, 2, 3, 1)
        y_nsa = _nsa(qt, kc, vct, ks, vst, kwn, vwt, gates)

        w_a2p = jnp.pad(gla_w_a2[l], ((0, LANES - gla_w_a2.shape[1]), (0, 0)))
        y_gla = _gla_mixer(hd, w_a2p, row(gla_b_a[l]), row(gla_norm_g[l]))

        xs = _merge(xs, g1, y_s5, y_nsa, y_gla, wgm, w_br_s5[l].astype(BF16), w_br_nsa[l].astype(BF16),
                    w_br_gla[l].astype(BF16), w_out[l].astype(BF16))
        xs = _ffn(xs, row(norm2_g[l]), ffn_w_up[l].astype(BF16), ffn_conv_w[l], row(ffn_conv_b[l]),
                  ffn_w_down[l].astype(BF16), row(final_g), final=(l == depth - 1))
    return xs.reshape(bsz, seq, d)
```

```python
import functools
import math

import jax
import jax.numpy as jnp
from jax import lax
from jax.experimental import pallas as pl
from jax.experimental.pallas import tpu as pltpu

F32 = jnp.float32
BF16 = jnp.bfloat16
I32 = jnp.int32

EPS = 1e-6
NEG_INF = -1e30

S5_GROUP = 16
S5_STATE = 64
NSA_HEADS = 8
NSA_KV_GROUPS = 2
NSA_HPG = NSA_HEADS // NSA_KV_GROUPS
HEAD_DIM = 64
ROPE_DIM = 16
ROPE_THETA = 500000.0
CMP_BLOCK = 32
CMP_STRIDE = 16
SLC_BLOCK = 64
SLC_TOPK = 16
WINDOW = 512
Q_BLOCK = 128
FORCE_BONUS = 1.0e4
GLA_HEADS = 4
GLA_DK = 32
GLA_DV = 64
GLA_TAU = 16.0
GLA_CHUNK = 64
GLA_SUPER = 256
CONV_WIDTH = 3

LANES = 128
SUBLANES = 8
KV_TILE = 512
CMP_CHUNK = 256
BF16_SUBLANES = 16
VMEM_LIMIT = 56 * 1024 * 1024


def _cparams(sem):
    return pltpu.CompilerParams(dimension_semantics=sem, vmem_limit_bytes=VMEM_LIMIT)


def _bdot(a, b):
    return jnp.dot(a.astype(BF16), b.astype(BF16), preferred_element_type=F32)


def _bdot_nt(a, b):
    return lax.dot_general(a.astype(BF16), b.astype(BF16), (((1,), (1,)), ((), ())),
                           preferred_element_type=F32)


def _hdot(a, b):
    return jnp.dot(a, b, preferred_element_type=F32, precision=lax.Precision.HIGHEST)


def _rms(x, g):
    return x * lax.rsqrt(jnp.mean(x * x, axis=-1, keepdims=True) + EPS) * g


def _rope_tab_kernel(pos_ref, invf_ref, c_ref, s1_ref, s2_ref):
    ang = pos_ref[...].astype(F32) * invf_ref[...]
    lane = lax.broadcasted_iota(I32, ang.shape, 1) % HEAD_DIM
    cosv = jnp.cos(ang)
    sinv = jnp.sin(ang)
    half = ROPE_DIM // 2
    c_ref[...] = jnp.where(lane < ROPE_DIM, cosv, 1.0)
    s1_ref[...] = jnp.where(lane < half, 0.0, jnp.where(lane < ROPE_DIM, sinv, 0.0))
    s2_ref[...] = jnp.where(lane < half, -sinv, 0.0)


def _rope_tables(positions):
    seq = positions.shape[0]
    tm = min(seq, 2048)
    inv_freq = ROPE_THETA ** (-jnp.arange(0, ROPE_DIM, 2, dtype=F32) / ROPE_DIM)
    lane = jnp.arange(LANES) % HEAD_DIM
    invf = jnp.where(lane < ROPE_DIM, inv_freq[lane % (ROPE_DIM // 2)], 0.0).reshape(1, LANES)
    tab = jax.ShapeDtypeStruct((seq, LANES), F32)
    return pl.pallas_call(
        _rope_tab_kernel,
        out_shape=(tab, tab, tab),
        grid=(seq // tm,),
        in_specs=[pl.BlockSpec((tm, 1), lambda i: (i, 0)),
                  pl.BlockSpec((1, LANES), lambda i: (0, 0))],
        out_specs=tuple(pl.BlockSpec((tm, LANES), lambda i: (i, 0)) for _ in range(3)),
        compiler_params=_cparams(("parallel",)),
        name="rope_tables",
    )(positions.reshape(seq, 1), invf)


def _proj_kernel(x_ref, g_ref, wa_ref, wb_ref, wc_ref, wd_ref, c_ref, s1_ref, s2_ref,
                 oa_ref, ob_ref, oc_ref, od_ref, *, q_width):
    xn = _rms(x_ref[...], g_ref[...]).astype(BF16)
    oa_ref[...] = jnp.dot(xn, wa_ref[...], preferred_element_type=F32)
    oc_ref[...] = jnp.dot(xn, wc_ref[...], preferred_element_type=F32).astype(BF16)
    od_ref[...] = jnp.dot(xn, wd_ref[...], preferred_element_type=F32)
    hb = jnp.dot(xn, wb_ref[...], preferred_element_type=F32)
    c = c_ref[...]
    s1 = s1_ref[...]
    s2 = s2_ref[...]
    half = ROPE_DIM // 2
    for j in range(hb.shape[1] // LANES):
        piece = hb[:, j * LANES:(j + 1) * LANES]
        rot = (piece * c + pltpu.roll(piece, half, 1) * s1
               + pltpu.roll(piece, LANES - half, 1) * s2)
        if j * LANES < q_width:
            rot = rot * (HEAD_DIM ** -0.5 * math.log2(math.e))
        ob_ref[:, j * LANES:(j + 1) * LANES] = rot.astype(BF16)


def _project(x, g, wa, wb, wc, wd, tabs, q_width):
    seq, d = x.shape
    tm = min(seq, 512)
    row = lambda w: pl.BlockSpec((tm, w), lambda i: (i, 0))
    full = lambda a: pl.BlockSpec(a.shape, lambda i: (0, 0))
    return pl.pallas_call(
        functools.partial(_proj_kernel, q_width=q_width),
        out_shape=(jax.ShapeDtypeStruct((seq, wa.shape[1]), F32),
                   jax.ShapeDtypeStruct((seq, wb.shape[1]), BF16),
                   jax.ShapeDtypeStruct((seq, wc.shape[1]), BF16),
                   jax.ShapeDtypeStruct((seq, wd.shape[1]), F32)),
        grid=(seq // tm,),
        in_specs=[row(d), full(g), full(wa), full(wb), full(wc), full(wd),
                  row(LANES), row(LANES), row(LANES)],
        out_specs=(row(wa.shape[1]), row(wb.shape[1]), row(wc.shape[1]), row(wd.shape[1])),
        compiler_params=_cparams(("parallel",)),
        name="in_proj",
    )(x, g, wa, wb, wc, wd, *tabs)


def _s5_prep_kernel(lr_ref, li_ref, ldt_ref, btr_ref, bti_ref, ctr_ref, cti_ref,
                    bre_ref, bim_ref, cre_ref, cim_ref, tab_ref):
    lr = lr_ref[...]
    li = li_ref[...]
    dt = jnp.exp(ldt_ref[...])
    mag = jnp.exp(lr * dt)
    ar = mag * jnp.cos(li * dt)
    ai = mag * jnp.sin(li * dt)
    nr = ar - 1.0
    ni = ai
    den = lr * lr + li * li
    f_re = (nr * lr + ni * li) / den
    f_im = (ni * lr - nr * li) / den
    btr = btr_ref[...]
    bti = bti_ref[...]
    rows = lax.broadcasted_iota(I32, btr.shape, 0) // S5_GROUP
    cols = lax.broadcasted_iota(I32, btr.shape, 1) // S5_STATE
    diag = rows == cols
    bre_ref[...] = jnp.where(diag, f_re * btr - f_im * bti, 0.0).astype(BF16)
    bim_ref[...] = jnp.where(diag, f_re * bti + f_im * btr, 0.0).astype(BF16)
    cre_ref[...] = jnp.where(diag, ctr_ref[...], 0.0).astype(BF16)
    cim_ref[...] = jnp.where(diag, cti_ref[...], 0.0).astype(BF16)
    pr = [ar]
    pi = [ai]
    for _ in range(SUBLANES - 1):
        pr_n = pr[-1] * ar - pi[-1] * ai
        pi_n = pr[-1] * ai + pi[-1] * ar
        pr.append(pr_n)
        pi.append(pi_n)
    rid = lax.broadcasted_iota(I32, (SUBLANES, lr.shape[1]), 0)
    zero = jnp.zeros((SUBLANES, lr.shape[1]), F32)
    p_re = zero
    p_im = zero
    for k in range(SUBLANES):
        p_re = jnp.where(rid == k, pr[k], p_re)
        p_im = jnp.where(rid == k, pi[k], p_im)
    for n, k in enumerate((1, 2, 4)):
        tab_ref[2 * n] = jnp.where(rid >= k, pr[k - 1], 0.0)
        tab_ref[2 * n + 1] = jnp.where(rid >= k, pi[k - 1], 0.0)
    tab_ref[6] = p_re
    tab_ref[7] = p_im


def _s5_prep(lam_re, lam_im, log_dt, b_re, b_im, c_re, c_im):
    g, p = lam_re.shape
    h = b_re.shape[-1]
    gp = g * p
    rowv = lambda a: a.reshape(1, gp)
    tile_t = lambda a: jnp.tile(a.reshape(h, gp), (g, 1))
    btr = tile_t(jnp.transpose(b_re, (2, 0, 1)))
    bti = tile_t(jnp.transpose(b_im, (2, 0, 1)))
    ctr = tile_t(jnp.transpose(c_re, (1, 0, 2)))
    cti = tile_t(jnp.transpose(c_im, (1, 0, 2)))
    mat = jax.ShapeDtypeStruct((g * h, gp), BF16)
    return pl.pallas_call(
        _s5_prep_kernel,
        out_shape=(mat, mat, mat, mat, jax.ShapeDtypeStruct((8, SUBLANES, gp), F32)),
        name="s5_prep",
    )(rowv(lam_re), rowv(lam_im), rowv(jnp.repeat(log_dt, p)), btr, bti, ctr, cti)


def _s5_kernel(u_ref, bre_ref, bim_ref, cre_ref, cim_ref, tab_ref, d_ref, wg_ref, bg_ref,
               y_ref, xr_ref, xi_ref, car_ref):
    @pl.when(pl.program_id(0) == 0)
    def _():
        car_ref[...] = jnp.zeros_like(car_ref)

    u = u_ref[...]
    ub = u.astype(BF16)
    xr_ref[...] = jnp.dot(ub, bre_ref[...], preferred_element_type=F32)
    xi_ref[...] = jnp.dot(ub, bim_ref[...], preferred_element_type=F32)
    tb = u.shape[0]

    def slab(r, carry):
        cr, ci = carry
        off = pl.multiple_of(r * SUBLANES, SUBLANES)
        xr = xr_ref[pl.ds(off, SUBLANES), :]
        xi = xi_ref[pl.ds(off, SUBLANES), :]
        for n, k in enumerate((1, 2, 4)):
            tr = tab_ref[2 * n]
            ti = tab_ref[2 * n + 1]
            sr = pltpu.roll(xr, k, 0)
            si = pltpu.roll(xi, k, 0)
            xr, xi = xr + tr * sr - ti * si, xi + tr * si + ti * sr
        pr = tab_ref[6]
        pi = tab_ref[7]
        xr, xi = xr + pr * cr - pi * ci, xi + pr * ci + pi * cr
        xr_ref[pl.ds(off, SUBLANES), :] = xr
        xi_ref[pl.ds(off, SUBLANES), :] = xi
        return xr[SUBLANES - 1:SUBLANES, :], xi[SUBLANES - 1:SUBLANES, :]

    cr, ci = lax.fori_loop(0, tb // SUBLANES, slab, (car_ref[0:1, :], car_ref[1:2, :]))
    car_ref[0:1, :] = cr
    car_ref[1:2, :] = ci
    y = _bdot_nt(xr_ref[...], cre_ref[...]) - _bdot_nt(xi_ref[...], cim_ref[...])
    y = y + d_ref[...] * u
    y = jax.nn.gelu(y)
    y = y * jax.nn.sigmoid(_bdot(y, wg_ref[...]) + bg_ref[...])
    y_ref[...] = y.astype(y_ref.dtype)


def _s5_mixer(u, prep, d_skip, w_glu, b_glu):
    seq, w = u.shape
    bre, bim, cre, cim, tab = prep
    gp = bre.shape[1]
    tb = min(seq, 512)
    full = lambda a: pl.BlockSpec(a.shape, lambda i: (0,) * a.ndim)
    return pl.pallas_call(
        _s5_kernel,
        out_shape=jax.ShapeDtypeStruct((seq, w), BF16),
        grid=(seq // tb,),
        in_specs=[pl.BlockSpec((tb, w), lambda i: (i, 0)), full(bre), full(bim), full(cre),
                  full(cim), full(tab), full(d_skip), full(w_glu), full(b_glu)],
        out_specs=pl.BlockSpec((tb, w), lambda i: (i, 0)),
        scratch_shapes=[pltpu.VMEM((tb, gp), F32), pltpu.VMEM((tb, gp), F32),
                        pltpu.VMEM((SUBLANES, gp), F32)],
        compiler_params=_cparams(("arbitrary",)),
        name="s5_mixer",
    )(u, bre, bim, cre, cim, tab, d_skip, w_glu, b_glu)


def _compress_kernel(r_ref, w1_ref, w2_ref, pe_ref, o_ref):
    r = r_ref[0, 0]
    half = r.shape[1]
    w1 = w1_ref[0]
    a = jnp.dot(r, w1[:half], preferred_element_type=F32)
    b = jnp.dot(r, w1[half:], preferred_element_type=F32)
    pew = _bdot(pe_ref[0], w1)
    nc = r.shape[0]
    hid = a + pltpu.roll(b, nc - 1, 0) + pew[0:1, :]
    o_ref[0, 0] = _bdot(jax.nn.gelu(hid), w2_ref[0]).astype(o_ref.dtype)


def _compress(r, w1, w2, pe):
    two, g, nc, wid = r.shape
    dh = w2.shape[-1]
    return pl.pallas_call(
        _compress_kernel,
        out_shape=jax.ShapeDtypeStruct((two, g, nc, dh), BF16),
        grid=(two, g),
        in_specs=[pl.BlockSpec((1, 1, nc, wid), lambda a, b: (a, b, 0, 0)),
                  pl.BlockSpec((1,) + w1.shape[1:], lambda a, b: (a, 0, 0)),
                  pl.BlockSpec((1,) + w2.shape[1:], lambda a, b: (a, 0, 0)),
                  pl.BlockSpec((1,) + pe.shape[1:], lambda a, b: (a, 0, 0))],
        out_specs=pl.BlockSpec((1, 1, nc, dh), lambda a, b: (a, b, 0, 0)),
        compiler_params=_cparams(("parallel", "parallel")),
        name="compress",
    )(r, w1, w2, pe)


def _nsa_pair_kernel(qt_ref, kc_ref, vct_ref, ks_ref, vst_ref, kw_ref, vwt_ref, g_ref,
                     out_ref, sc_ref, imp_ref, selb_ref, sa_ref, sb_ref, *, nc, ns):
    qi = pl.program_id(0)
    groups = range(qt_ref.shape[0])
    qts = [qt_ref[g, 0] for g in groups]
    dh, width = qts[0].shape
    vrows = vst_ref.shape[1]
    lane = lax.broadcasted_iota(I32, (1, width), 1)
    t_row = qi * Q_BLOCK + (lane % Q_BLOCK)
    tq = qi * Q_BLOCK + lax.broadcasted_iota(I32, (1, Q_BLOCK), 1)
    all_heads = lambda a: jnp.concatenate([a] * NSA_HPG, axis=1)
    fdot = lambda a, b: jnp.dot(a, b, preferred_element_type=F32)

    pad = SUBLANES
    ch = min(CMP_CHUNK, nc)
    n_ch = (((qi + 1) * Q_BLOCK - CMP_BLOCK) // CMP_STRIDE) // ch + 1

    def cmp_scores(c, ms):
        off = pl.multiple_of(c * ch, ch)
        ci = off + lax.broadcasted_iota(I32, (ch, 1), 0)
        vis = ci * CMP_STRIDE + (CMP_BLOCK - 1) <= t_row
        out = []
        for g in groups:
            s = jnp.where(vis, fdot(kc_ref[g, pl.ds(off, ch), :], qts[g]), NEG_INF)
            sc_ref[g, pl.ds(off, ch), :] = s
            out.append(jnp.maximum(ms[g], jnp.max(s, axis=0, keepdims=True)))
        return tuple(out)

    ms = lax.fori_loop(0, n_ch, cmp_scores, tuple(jnp.full((1, width), NEG_INF, F32) for _ in groups))
    ms = [jnp.maximum(m, 0.1 * NEG_INF) for m in ms]

    def cmp_probs(c, accs):
        off = pl.multiple_of(c * ch, ch)
        out = []
        for g in groups:
            e = jnp.exp2(sc_ref[g, pl.ds(off, ch), :] - ms[g])
            sc_ref[g, pl.ds(off, ch), :] = e
            out.append(accs[g] + fdot(vct_ref[g, :, pl.ds(off, ch)], e.astype(BF16)))
        return tuple(out)

    accs = lax.fori_loop(0, n_ch, cmp_probs, tuple(jnp.zeros((vrows, width), F32) for _ in groups))
    inv_l = [1.0 / jnp.maximum(a[dh:dh + 1, :], 1e-30) for a in accs]
    o_cmp = [a[0:dh, :] * i for a, i in zip(accs, inv_l)]
    imp_ref[...] = jnp.zeros_like(imp_ref)

    def cmp_importance(c, _):
        off = pl.multiple_of(c * ch, ch)
        for g in groups:
            p = sc_ref[g, pl.ds(off, ch), :] * inv_l[g]
            imp = p[:, 0:Q_BLOCK]
            for h in range(1, NSA_HPG):
                imp = imp + p[:, h * Q_BLOCK:(h + 1) * Q_BLOCK]
            imp_ref[g, pl.ds(pl.multiple_of(pad + off, pad), ch), :] = imp
        return 0

    lax.fori_loop(0, n_ch, cmp_importance, 0)

    span_w = WINDOW + Q_BLOCK
    woff = pl.multiple_of(qi * Q_BLOCK, Q_BLOCK)
    r = lax.broadcasted_iota(I32, (span_w, 1), 0)
    dpos = lax.broadcasted_iota(I32, (1, Q_BLOCK), 1) + WINDOW - r
    ok = (dpos >= 0) & (dpos < WINDOW) & (qi * Q_BLOCK - WINDOW + r >= 0)
    win_bias = all_heads(jnp.where(ok, 0.0, NEG_INF))
    o_win = []
    for g in groups:
        sa_ref[g] = fdot(ks_ref[g, 0:KV_TILE, :], qts[g])
        s = fdot(kw_ref[g, pl.ds(woff, span_w), :], qts[g]) + win_bias
        p = jnp.exp2(s - jnp.max(s, axis=0, keepdims=True)).astype(BF16)
        acc = fdot(vwt_ref[g, :, pl.ds(woff, span_w)], p)
        o_win.append(acc[0:dh, :] / jnp.maximum(acc[dh:dh + 1, :], 1e-30))

    ratio = SLC_BLOCK // CMP_STRIDE
    span = ratio + CMP_BLOCK // CMP_STRIDE - 1
    lead = CMP_BLOCK // CMP_STRIDE - 1
    jb = lax.broadcasted_iota(I32, (ns, 1), 0)
    cur = tq // SLC_BLOCK
    bonus = ((jb == 0) | (jb == cur) | (jb == cur - 1)).astype(F32) * FORCE_BONUS
    jf = jb.astype(F32)
    for g in groups:
        imp_slc = imp_ref[g, pl.ds(pad - lead, ns, stride=ratio), :]
        for sft in range(1, span):
            imp_slc = imp_slc + imp_ref[g, pl.ds(pad - lead + sft, ns, stride=ratio), :]
        score = jnp.where(jb * SLC_BLOCK <= tq, imp_slc + bonus, NEG_INF)
        for _ in range(min(SLC_TOPK, ns)):
            mx = jnp.max(score, axis=0, keepdims=True)
            first = jnp.min(jnp.where(score == mx, jf, float(ns)), axis=0, keepdims=True)
            score = jnp.where(jf == first, -jnp.inf, score)
        selb_ref[g] = jnp.where(score == -jnp.inf, 0.0, NEG_INF)

    blocks = KV_TILE // SLC_BLOCK

    def causal(kt, visible):
        kpos = kt * KV_TILE + lax.broadcasted_iota(I32, (KV_TILE, 1), 0)
        return jnp.where(kpos <= tq, visible, NEG_INF)

    def tile_bias(g, kt, is_causal):
        bias_t = selb_ref[g, pl.ds(pl.multiple_of(kt * blocks, blocks), blocks), :]
        bias = jnp.concatenate(
            [jnp.broadcast_to(bias_t[b:b + 1, :], (SLC_BLOCK, Q_BLOCK)) for b in range(blocks)], axis=0)
        return all_heads(causal(kt, bias) if is_causal else bias)

    def scores(kt, dst_ref, is_causal):
        off = pl.multiple_of(kt * KV_TILE, KV_TILE)
        for g in groups:
            dst_ref[g] = fdot(ks_ref[g, pl.ds(off, KV_TILE), :], qts[g]) + tile_bias(g, kt, is_causal)

    def consume(kt, src_ref, carry, extra_bias=None):
        off = pl.multiple_of(kt * KV_TILE, KV_TILE)
        out = []
        for g in groups:
            m, acc = carry[g]
            s = src_ref[g]
            if extra_bias is not None:
                s = s + extra_bias
            m_new = jnp.maximum(m, jnp.max(s, axis=0, keepdims=True))
            p = jnp.exp2(s - m_new).astype(BF16)
            out.append((m_new, jnp.exp2(m - m_new) * acc + fdot(vst_ref[g, :, pl.ds(off, KV_TILE)], p)))
        return tuple(out)

    n_before = (qi * Q_BLOCK) // KV_TILE
    for g in groups:
        sa_ref[g] = sa_ref[g] + tile_bias(g, 0, False)

    def tile_pair(i, carry):
        scores(2 * i + 1, sb_ref, False)
        carry = consume(2 * i, sa_ref, carry)
        scores(2 * i + 2, sa_ref, False)
        return consume(2 * i + 1, sb_ref, carry)

    init = tuple((jnp.full((1, width), NEG_INF, F32), jnp.zeros((vrows, width), F32)) for _ in groups)
    carry = lax.fori_loop(0, n_before // 2, tile_pair, init)
    k0 = 2 * (n_before // 2)
    scores(k0 + 1, sb_ref, True)
    carry = consume(k0, sa_ref, carry, all_heads(causal(k0, 0.0)))
    carry = consume(k0 + 1, sb_ref, carry)

    for g in groups:
        acc = carry[g][1]
        o_slc = acc[0:dh, :] / jnp.maximum(acc[dh:dh + 1, :], 1e-30)
        gate = jax.nn.sigmoid(g_ref[0, g])
        for h in range(NSA_HPG):
            hs = slice(h * Q_BLOCK, (h + 1) * Q_BLOCK)
            y = (gate[3 * h:3 * h + 1, :] * o_cmp[g][:, hs] + gate[3 * h + 1:3 * h + 2, :] * o_slc[:, hs]
                 + gate[3 * h + 2:3 * h + 3, :] * o_win[g][:, hs])
            row = (g * NSA_HPG + h) * dh
            out_ref[row:row + dh, :] = y.astype(out_ref.dtype)


def _nsa_pair(qt, kc, vct, ks, vst, kw, vwt, gates):
    g, nqt, dh, width = qt.shape
    seq = ks.shape[1]
    nc = kc.shape[1]
    ns = seq // SLC_BLOCK
    resident = lambda a: pl.BlockSpec(a.shape, lambda qi: (0,) * a.ndim, pipeline_mode=pl.Buffered(1))
    return pl.pallas_call(
        functools.partial(_nsa_pair_kernel, nc=nc, ns=ns),
        out_shape=jax.ShapeDtypeStruct((g * NSA_HPG * dh, seq), BF16),
        grid=(nqt,),
        in_specs=[pl.BlockSpec((g, 1, dh, width), lambda qi: (0, qi, 0, 0)),
                  resident(kc), resident(vct), resident(ks), resident(vst), resident(kw), resident(vwt),
                  pl.BlockSpec((1,) + gates.shape[1:], lambda qi: (qi, 0, 0, 0))],
        out_specs=pl.BlockSpec((g * NSA_HPG * dh, Q_BLOCK), lambda qi: (0, qi)),
        scratch_shapes=[pltpu.VMEM((g, nc, width), F32),
                        pltpu.VMEM((g, nc + 2 * SUBLANES, Q_BLOCK), F32),
                        pltpu.VMEM((g, ns, Q_BLOCK), F32),
                        pltpu.VMEM((g, KV_TILE, width), F32),
                        pltpu.VMEM((g, KV_TILE, width), F32)],
        compiler_params=_cparams(("arbitrary",)),
        name="sparse_attention",
    )(qt, kc, vct, ks, vst, kw, vwt, gates)


def _split_bf16(x):
    hi = x.astype(BF16)
    return hi, (x - hi.astype(F32)).astype(BF16)


def _gla_kernel(h_ref, wa_ref, ba_ref, ng_ref, y_ref, st_ref):
    @pl.when(pl.program_id(0) == 0)
    def _():
        st_ref[...] = jnp.zeros_like(st_ref)

    kw = GLA_HEADS * GLA_DK
    vw = GLA_HEADS * GLA_DV
    c = GLA_CHUNK
    sb = min(GLA_SUPER, h_ref.shape[0])
    ri = lax.broadcasted_iota(I32, (sb, sb), 0)
    cj = lax.broadcasted_iota(I32, (sb, sb), 1)
    same_chunk = (ri // c) == (cj // c)
    tril = same_chunk & (ri >= cj)
    tril_m = jnp.where(tril, 1.0, 0.0).astype(BF16)
    chunk_m = jnp.where(same_chunk, 1.0, 0.0).astype(BF16)
    ones_c = jnp.ones((c, LANES), BF16)
    krow = lax.broadcasted_iota(I32, (kw, vw), 0) // GLA_DK
    vcol = lax.broadcasted_iota(I32, (kw, vw), 1) // GLA_DV
    blockdiag = krow == vcol
    klane = lax.broadcasted_iota(I32, (1, kw), 1) // GLA_DK
    vlane = lax.broadcasted_iota(I32, (1, vw), 1) // GLA_DV
    va = lax.broadcasted_iota(I32, (vw, vw), 0) // GLA_DV
    vb = lax.broadcasted_iota(I32, (vw, vw), 1) // GLA_DV
    head_avg = jnp.where(va == vb, 1.0 / GLA_DV, 0.0).astype(BF16)
    tn = (((0,), (0,)), ((), ()))

    state = st_ref[...]
    for blk in range(h_ref.shape[0] // sb):
        rows = slice(blk * sb, (blk + 1) * sb)
        q = h_ref[rows, 0:kw] * (GLA_DK ** -0.5)
        k = h_ref[rows, kw:2 * kw]
        v = h_ref[rows, 2 * kw:2 * kw + vw].astype(BF16)
        r = h_ref[rows, 2 * kw + vw:2 * kw + 2 * vw]
        a_low = h_ref[rows, 2 * kw + 2 * vw:2 * kw + 2 * vw + LANES]
        la = jax.nn.log_sigmoid(_hdot(a_low, wa_ref[...]) + ba_ref[...]) / GLA_TAU
        la_hi, la_lo = _split_bf16(la)
        sum_la = lambda mat: (jnp.dot(mat, la_hi, preferred_element_type=F32)
                              + jnp.dot(mat, la_lo, preferred_element_type=F32))
        bcum = sum_la(tril_m)
        b_last = sum_la(chunk_m)
        q_t = (q * jnp.exp(bcum)).astype(BF16)
        k_t = (k * jnp.exp(-bcum)).astype(BF16)
        k_d = (k * jnp.exp(b_last - bcum)).astype(BF16)
        o = jnp.zeros((sb, vw), F32)
        for hd in range(GLA_HEADS):
            attn = _bdot_nt(jnp.where(klane == hd, q_t, jnp.zeros_like(q_t)), k_t)
            attn = jnp.where(tril, attn, 0.0).astype(BF16)
            o = o + jnp.where(vlane == hd, jnp.dot(attn, v, preferred_element_type=F32), 0.0)
        inter = []
        for n in range(sb // c):
            cs = slice(n * c, (n + 1) * c)
            inter.append(jnp.dot(q_t[cs], state.astype(BF16), preferred_element_type=F32))
            kv = jnp.where(blockdiag, lax.dot_general(k_d[cs], v[cs], tn, preferred_element_type=F32), 0.0)
            tot = (lax.dot_general(la_hi[cs], ones_c, tn, preferred_element_type=F32)
                   + lax.dot_general(la_lo[cs], ones_c, tn, preferred_element_type=F32))
            decay = jnp.exp(tot)
            state = jnp.concatenate([decay] * (vw // LANES), axis=1) * state + kv
        o = o + jnp.concatenate(inter, axis=0)
        sq_hi, sq_lo = _split_bf16(o * o)
        ms = (jnp.dot(sq_hi, head_avg, preferred_element_type=F32)
              + jnp.dot(sq_lo, head_avg, preferred_element_type=F32))
        o = o * lax.rsqrt(ms + EPS)
        y_ref[rows, :] = (o * ng_ref[...] * (r * jax.nn.sigmoid(r))).astype(y_ref.dtype)
    st_ref[...] = state


def _gla_mixer(hd, w_a2p, b_a, norm_g):
    seq, wid = hd.shape
    tb = min(seq, 512)
    kw = GLA_HEADS * GLA_DK
    vw = GLA_HEADS * GLA_DV
    full = lambda a: pl.BlockSpec(a.shape, lambda i: (0, 0))
    return pl.pallas_call(
        _gla_kernel,
        out_shape=jax.ShapeDtypeStruct((seq, vw), BF16),
        grid=(seq // tb,),
        in_specs=[pl.BlockSpec((tb, wid), lambda i: (i, 0)), full(w_a2p), full(b_a), full(norm_g)],
        out_specs=pl.BlockSpec((tb, vw), lambda i: (i, 0)),
        scratch_shapes=[pltpu.VMEM((kw, vw), F32)],
        compiler_params=_cparams(("arbitrary",)),
        name="gla_mixer",
    )(hd, w_a2p, b_a, norm_g)


def _merge_kernel(x_ref, g_ref, ys_ref, yn_ref, yg_ref, wgm_ref, ws_ref, wn_ref, wg_ref, wo_ref, o_ref):
    x = x_ref[...]
    d = x.shape[1]
    xn = _rms(x, g_ref[...]).astype(BF16)
    gm = jax.nn.sigmoid(jnp.dot(xn, wgm_ref[...], preferred_element_type=F32))
    mixed = (gm[:, 0:d] * jnp.dot(ys_ref[...], ws_ref[...], preferred_element_type=F32)
             + gm[:, d:2 * d] * lax.dot_general(yn_ref[...], wn_ref[...], (((0,), (0,)), ((), ())),
                                                preferred_element_type=F32)
             + gm[:, 2 * d:3 * d] * jnp.dot(yg_ref[...], wg_ref[...], preferred_element_type=F32))
    o_ref[...] = x + _bdot(mixed, wo_ref[...])


def _merge(x, g, ys, yn, yg, wgm, ws, wn, wg, wo):
    seq, d = x.shape
    tm = min(seq, 512)
    row = lambda a: pl.BlockSpec((tm, a.shape[1]), lambda i: (i, 0))
    full = lambda a: pl.BlockSpec(a.shape, lambda i: (0, 0))
    return pl.pallas_call(
        _merge_kernel,
        out_shape=jax.ShapeDtypeStruct((seq, d), F32),
        grid=(seq // tm,),
        in_specs=[row(x), full(g), row(ys), pl.BlockSpec((yn.shape[0], tm), lambda i: (0, i)), row(yg),
                  full(wgm), full(ws), full(wn), full(wg), full(wo)],
        out_specs=row(x),
        compiler_params=_cparams(("parallel",)),
        name="merge",
    )(x, g, ys, yn, yg, wgm, ws, wn, wg, wo)


def _ffn_kernel(x_ref, xp_ref, g_ref, wug_ref, wuv_ref, cwg_ref, cwv_ref, cbg_ref, cbv_ref, wd_ref,
                fg_ref, o_ref, xn_ref, *, final):
    i = pl.program_id(0)
    j = pl.program_id(1)
    tm = x_ref.shape[0]
    halo = xp_ref.shape[0]

    @pl.when(j == 0)
    def _():
        xn_ref[0:halo, :] = jnp.where(i == 0, 0.0, _rms(xp_ref[...], g_ref[...])).astype(BF16)
        xn_ref[halo:halo + tm, :] = _rms(x_ref[...], g_ref[...]).astype(BF16)

    xn = xn_ref[...]

    def conv(w_ref, cw_ref, cb_ref):
        h = jnp.dot(xn, w_ref[...], preferred_element_type=F32)
        hc = cb_ref[...]
        for t in range(CONV_WIDTH):
            sh = CONV_WIDTH - 1 - t
            hs = h if sh == 0 else pltpu.roll(h, sh, 0)
            hc = hc + cw_ref[t:t + 1, :] * hs[halo:halo + tm, :]
        return hc

    act = jax.nn.gelu(conv(wug_ref, cwg_ref, cbg_ref)) * conv(wuv_ref, cwv_ref, cbv_ref)
    part = _bdot(act, wd_ref[...])

    @pl.when(j == 0)
    def _():
        o_ref[...] = x_ref[...] + part

    @pl.when(j > 0)
    def _():
        o_ref[...] = o_ref[...] + part

    if final:
        @pl.when(j == pl.num_programs(1) - 1)
        def _():
            o_ref[...] = _rms(o_ref[...], fg_ref[...])


def _ffn(x, g, w_up, conv_w, conv_b, w_down, final_g, final):
    seq, d = x.shape
    dff = w_down.shape[0]
    tm = min(seq, 1024)
    nj = 2
    tn = dff // nj
    halo = SUBLANES
    hb = tm // halo
    return pl.pallas_call(
        functools.partial(_ffn_kernel, final=final),
        out_shape=jax.ShapeDtypeStruct((seq, d), F32),
        grid=(seq // tm, nj),
        in_specs=[pl.BlockSpec((tm, d), lambda i, j: (i, 0)),
                  pl.BlockSpec((halo, d), lambda i, j: (jnp.maximum(i * hb - 1, 0), 0)),
                  pl.BlockSpec((1, d), lambda i, j: (0, 0)),
                  pl.BlockSpec((d, tn), lambda i, j: (0, j)),
                  pl.BlockSpec((d, tn), lambda i, j: (0, nj + j)),
                  pl.BlockSpec((CONV_WIDTH, tn), lambda i, j: (0, j)),
                  pl.BlockSpec((CONV_WIDTH, tn), lambda i, j: (0, nj + j)),
                  pl.BlockSpec((1, tn), lambda i, j: (0, j)),
                  pl.BlockSpec((1, tn), lambda i, j: (0, nj + j)),
                  pl.BlockSpec((tn, d), lambda i, j: (j, 0)),
                  pl.BlockSpec((1, d), lambda i, j: (0, 0))],
        out_specs=pl.BlockSpec((tm, d), lambda i, j: (i, 0)),
        scratch_shapes=[pltpu.VMEM((halo + tm, d), BF16)],
        compiler_params=_cparams(("parallel", "arbitrary")),
        name="conv_ffn",
    )(x, x, g, w_up, w_up, conv_w, conv_w, conv_b, conv_b, w_down, final_g)


def _split_w_in(w):
    d = w.shape[0]
    s5w = d // 4
    nsa_w = NSA_HEADS * HEAD_DIM
    kvw = NSA_KV_GROUPS * HEAD_DIM
    kw = GLA_HEADS * GLA_DK
    vw = GLA_HEADS * GLA_DV
    sizes = (s5w, nsa_w, kvw, kvw, kvw, kvw, kvw, kvw, 3 * NSA_HEADS, kw, kw, vw, 16, vw, 3 * d)
    assert sum(sizes) == w.shape[1]
    offs = [0]
    for sz in sizes:
        offs.append(offs[-1] + sz)
    col = lambda n: w[:, offs[n]:offs[n + 1]]
    (u, q, kc, vc, ks, vs, kwn, vwn, gn, gq, gk, gv, ga, gr, gm) = [col(n) for n in range(len(sizes))]
    padl = lambda a: jnp.pad(a, ((0, 0), (0, LANES - a.shape[1])))
    wa = u
    wb = jnp.concatenate([q, kc, ks, kwn], axis=1)
    wc = jnp.concatenate([vc, vs, vwn], axis=1)
    wd = jnp.concatenate([gq, gk, gv, gr, padl(ga), padl(gn)], axis=1)
    return [a.astype(BF16) for a in (wa, wb, wc, wd, gm)]


def kernel(x, positions, norm1_g, w_in, s5_lam_re, s5_lam_im, s5_log_dt, s5_b_re, s5_b_im, s5_c_re, s5_c_im, s5_d, s5_w_glu, s5_b_glu, nsa_pe_k, nsa_pe_v, nsa_ck_w1, nsa_ck_w2, nsa_cv_w1, nsa_cv_w2, gla_w_a2, gla_b_a, gla_norm_g, w_br_s5, w_br_nsa, w_br_gla, w_out, norm2_g, ffn_w_up, ffn_conv_w, ffn_conv_b, ffn_w_down, final_g):
    bsz, seq, d = x.shape
    depth = w_in.shape[0]
    assert bsz == 1 and seq % KV_TILE == 0 and d % LANES == 0
    g = NSA_KV_GROUPS
    dh = HEAD_DIM
    nqt = seq // Q_BLOCK
    nc = seq // CMP_STRIDE
    nsa_w = NSA_HEADS * dh
    kvw = g * dh
    kw = GLA_HEADS * GLA_DK
    vw = GLA_HEADS * GLA_DV
    row = lambda a: a.reshape(1, -1)

    xs = x.reshape(seq, d)
    tabs = _rope_tables(positions.reshape(seq))

    for l in range(depth):
        wa, wb, wc, wd, wgm = _split_w_in(w_in[l])
        g1 = row(norm1_g[l])
        u_s5, hb, hc, hd = _project(xs, g1, wa, wb, wc, wd, tabs, nsa_w)

        prep = _s5_prep(s5_lam_re[l], s5_lam_im[l], s5_log_dt[l], s5_b_re[l], s5_b_im[l],
                        s5_c_re[l], s5_c_im[l])
        y_s5 = _s5_mixer(u_s5, prep, row(s5_d[l]), s5_w_glu[l].astype(BF16), row(s5_b_glu[l]))

        q = hb[:, :nsa_w]
        qt = q.reshape(nqt, Q_BLOCK, g, NSA_HPG, dh).transpose(2, 0, 4, 3, 1).reshape(g, nqt, dh, NSA_HPG * Q_BLOCK)
        per_group = lambda a: a.reshape(seq, g, dh).transpose(1, 0, 2)
        to_rows = lambda a: per_group(a).reshape(g, nc, CMP_STRIDE * dh)
        k_cmp, k_slc, k_win = (hb[:, nsa_w + n * kvw:nsa_w + (n + 1) * kvw] for n in range(3))
        v_cmp, v_slc, v_win = (hc[:, n * kvw:(n + 1) * kvw] for n in range(3))
        r = jnp.stack([to_rows(k_cmp), to_rows(v_cmp)])
        pe = jnp.stack([nsa_pe_k[l].reshape(1, -1), nsa_pe_v[l].reshape(1, -1)])
        pe = jnp.broadcast_to(pe, (2, SUBLANES, pe.shape[-1]))
        w1 = jnp.stack([nsa_ck_w1[l], nsa_cv_w1[l]]).astype(BF16)
        w2 = jnp.stack([nsa_ck_w2[l], nsa_cv_w2[l]]).astype(BF16)
        cmp = _compress(r, w1, w2, pe)
        def values_t(v):
            n = v.shape[1]
            return jnp.concatenate([v.transpose(0, 2, 1), jnp.ones((g, 1, n), BF16),
                                    jnp.zeros((g, BF16_SUBLANES - 1, n), BF16)], axis=1)

        kc = cmp[0]
        vct = values_t(cmp[1])
        ks = per_group(k_slc)
        vst = values_t(per_group(v_slc))
        kwn = jnp.pad(per_group(k_win), ((0, 0), (WINDOW, 0), (0, 0)))
        vwt = values_t(jnp.pad(per_group(v_win), ((0, 0), (WINDOW, 0), (0, 0))))
        gates = hd[:, 2 * kw + 2 * vw + LANES:2 * kw + 2 * vw + LANES + 3 * NSA_HEADS]
        gates = gates.reshape(nqt, Q_BLOCK, g, 3 * NSA_HPG).transpose(0, 2, 3, 1)
        y_nsa = _nsa_pair(qt, kc, vct, ks, vst, kwn, vwt, gates)

        w_a2p = jnp.pad(gla_w_a2[l], ((0, LANES - gla_w_a2.shape[1]), (0, 0)))
        y_gla = _gla_mixer(hd, w_a2p, row(gla_b_a[l]), row(gla_norm_g[l]))

        xs = _merge(xs, g1, y_s5, y_nsa, y_gla, wgm, w_br_s5[l].astype(BF16), w_br_nsa[l].astype(BF16),
                    w_br_gla[l].astype(BF16), w_out[l].astype(BF16))
        xs = _ffn(xs, row(norm2_g[l]), ffn_w_up[l].astype(BF16), ffn_conv_w[l], row(ffn_conv_b[l]),
                  ffn_w_down[l].astype(BF16), row(final_g), final=(l == depth - 1))
    return xs.reshape(bsz, seq, d)
```

```python
import functools
import math

import jax
import jax.numpy as jnp
from jax import lax
from jax.experimental import pallas as pl
from jax.experimental.pallas import tpu as pltpu

F32 = jnp.float32
BF16 = jnp.bfloat16
I32 = jnp.int32

EPS = 1e-6
NEG_INF = -1e30

S5_GROUP = 16
S5_STATE = 64
NSA_HEADS = 8
NSA_KV_GROUPS = 2
NSA_HPG = NSA_HEADS // NSA_KV_GROUPS
HEAD_DIM = 64
ROPE_DIM = 16
ROPE_THETA = 500000.0
CMP_BLOCK = 32
CMP_STRIDE = 16
SLC_BLOCK = 64
SLC_TOPK = 16
WINDOW = 512
Q_BLOCK = 128
FORCE_BONUS = 1.0e4
GLA_HEADS = 4
GLA_DK = 32
GLA_DV = 64
GLA_TAU = 16.0
GLA_CHUNK = 64
GLA_SUPER = 256
CONV_WIDTH = 3

LANES = 128
SUBLANES = 8
KV_TILE = 512
CMP_CHUNK = 256
BF16_SUBLANES = 16
VMEM_LIMIT = 56 * 1024 * 1024


def _cparams(sem):
    return pltpu.CompilerParams(dimension_semantics=sem, vmem_limit_bytes=VMEM_LIMIT)


def _bdot(a, b):
    return jnp.dot(a.astype(BF16), b.astype(BF16), preferred_element_type=F32)


def _bdot_nt(a, b):
    return lax.dot_general(a.astype(BF16), b.astype(BF16), (((1,), (1,)), ((), ())),
                           preferred_element_type=F32)


def _hdot(a, b):
    return jnp.dot(a, b, preferred_element_type=F32, precision=lax.Precision.HIGHEST)


def _rms(x, g):
    return x * lax.rsqrt(jnp.mean(x * x, axis=-1, keepdims=True) + EPS) * g


def _rope_tab_kernel(pos_ref, post_ref, invf_ref, invft_ref, c_ref, s1_ref, s2_ref, ct_ref, s1t_ref, s2t_ref):
    half = ROPE_DIM // 2

    def tables(ang, dim, c_out, s1_out, s2_out):
        cosv = jnp.cos(ang)
        sinv = jnp.sin(ang)
        c_out[...] = jnp.where(dim < ROPE_DIM, cosv, 1.0)
        s1_out[...] = jnp.where(dim < half, 0.0, jnp.where(dim < ROPE_DIM, sinv, 0.0))
        s2_out[...] = jnp.where(dim < half, -sinv, 0.0)

    ang = pos_ref[...].astype(F32) * invf_ref[...]
    tables(ang, lax.broadcasted_iota(I32, ang.shape, 1) % HEAD_DIM, c_ref, s1_ref, s2_ref)
    ang_t = invft_ref[...] * post_ref[...].astype(F32)
    tables(ang_t, lax.broadcasted_iota(I32, ang_t.shape, 0), ct_ref, s1t_ref, s2t_ref)


def _rope_tables(positions):
    seq = positions.shape[0]
    tm = min(seq, 2048)
    inv_freq = ROPE_THETA ** (-jnp.arange(0, ROPE_DIM, 2, dtype=F32) / ROPE_DIM)
    dim = jnp.arange(LANES) % HEAD_DIM
    invf = jnp.where(dim < ROPE_DIM, inv_freq[dim % (ROPE_DIM // 2)], 0.0)
    tab = jax.ShapeDtypeStruct((seq, LANES), F32)
    tab_t = jax.ShapeDtypeStruct((HEAD_DIM, seq), F32)
    return pl.pallas_call(
        _rope_tab_kernel,
        out_shape=(tab, tab, tab, tab_t, tab_t, tab_t),
        grid=(seq // tm,),
        in_specs=[pl.BlockSpec((tm, 1), lambda i: (i, 0)),
                  pl.BlockSpec((1, tm), lambda i: (0, i)),
                  pl.BlockSpec((1, LANES), lambda i: (0, 0)),
                  pl.BlockSpec((HEAD_DIM, 1), lambda i: (0, 0))],
        out_specs=tuple([pl.BlockSpec((tm, LANES), lambda i: (i, 0))] * 3
                        + [pl.BlockSpec((HEAD_DIM, tm), lambda i: (0, i))] * 3),
        compiler_params=_cparams(("parallel",)),
        name="rope_tables",
    )(positions.reshape(seq, 1), positions.reshape(1, seq), invf.reshape(1, LANES),
      invf[:HEAD_DIM].reshape(HEAD_DIM, 1))


def _proj_kernel(x_ref, g_ref, wu_ref, wk_ref, wvc_ref, wd_ref, wqt_ref, wvt_ref, wgt_ref,
                 c_ref, s1_ref, s2_ref, ct_ref, s1t_ref, s2t_ref,
                 u_ref, qt_ref, cmp_ref, ks_ref, kw_ref, vst_ref, vwt_ref, gt_ref, hd_ref):
    xn = _rms(x_ref[...], g_ref[...]).astype(BF16)
    tm = xn.shape[0]
    n_qt = tm // Q_BLOCK
    half = ROPE_DIM // 2
    fdot = lambda a, b: jnp.dot(a, b, preferred_element_type=F32)
    tdot = lambda w, a: lax.dot_general(w, a, (((1,), (1,)), ((), ())), preferred_element_type=F32)

    u_ref[...] = fdot(xn, wu_ref[...])
    hd_ref[...] = fdot(xn, wd_ref[...])
    cmp_ref[1] = fdot(xn, wvc_ref[...])
    hk = fdot(xn, wk_ref[...])
    c, s1, s2 = c_ref[...], s1_ref[...], s2_ref[...]
    for j, out in enumerate((cmp_ref.at[0], ks_ref, kw_ref)):
        piece = hk[:, j * LANES:(j + 1) * LANES]
        rot = piece * c + pltpu.roll(piece, half, 1) * s1 + pltpu.roll(piece, LANES - half, 1) * s2
        out[...] = rot.astype(out.dtype)

    q_t = tdot(wqt_ref[...], xn)
    ct, s1t, s2t = ct_ref[...], s1t_ref[...], s2t_ref[...]
    dh = ct.shape[0]
    for gh in range(q_t.shape[0] // dh):
        blk = q_t[gh * dh:(gh + 1) * dh, :]
        rot = blk * ct + pltpu.roll(blk, half, 0) * s1t + pltpu.roll(blk, dh - half, 0) * s2t
        rot = (rot * (dh ** -0.5 * math.log2(math.e))).astype(qt_ref.dtype)
        g, h = divmod(gh, NSA_HPG)
        for j in range(n_qt):
            qt_ref[g, j, :, h * Q_BLOCK:(h + 1) * Q_BLOCK] = rot[:, j * Q_BLOCK:(j + 1) * Q_BLOCK]

    v_t = tdot(wvt_ref[...], xn)
    rid = lax.broadcasted_iota(I32, (vst_ref.shape[1] - dh, tm), 0)
    tail = jnp.where(rid == 0, 1.0, 0.0).astype(vst_ref.dtype)
    for b, out in enumerate((vst_ref, vwt_ref)):
        for g in range(out.shape[0]):
            row = (b * out.shape[0] + g) * dh
            out[g, 0:dh, :] = v_t[row:row + dh, :].astype(out.dtype)
            out[g, dh:, :] = tail

    g_t = tdot(wgt_ref[...], xn)
    rows = gt_ref.shape[2]
    for j in range(n_qt):
        for g in range(gt_ref.shape[1]):
            gt_ref[j, g] = g_t[g * rows:(g + 1) * rows, j * Q_BLOCK:(j + 1) * Q_BLOCK]


def _project(x, g, w, tabs):
    seq, d = x.shape
    tm = min(seq, 512)
    n_qt = tm // Q_BLOCK
    groups = NSA_KV_GROUPS
    vrows = HEAD_DIM + BF16_SUBLANES
    gate_rows = w["gt"].shape[0] // groups
    row = lambda wid: pl.BlockSpec((tm, wid), lambda i: (i, 0))
    col = lambda r: pl.BlockSpec((r, tm), lambda i: (0, i))
    full = lambda a: pl.BlockSpec(a.shape, lambda i: (0, 0))
    weights = [w[k] for k in ("u", "k", "vc", "d", "qt", "vt", "gt")]
    sds = jax.ShapeDtypeStruct
    return pl.pallas_call(
        _proj_kernel,
        out_shape=(sds((seq, w["u"].shape[1]), F32),
                   sds((groups, seq // Q_BLOCK, HEAD_DIM, NSA_HPG * Q_BLOCK), BF16),
                   sds((2, seq, LANES), F32),
                   sds((seq, LANES), BF16), sds((seq, LANES), BF16),
                   sds((groups, vrows, seq), BF16), sds((groups, vrows, seq), BF16),
                   sds((seq // Q_BLOCK, groups, gate_rows, Q_BLOCK), F32),
                   sds((seq, w["d"].shape[1]), F32)),
        grid=(seq // tm,),
        in_specs=[row(d), full(g)] + [full(a) for a in weights]
                 + [row(LANES)] * 3 + [col(HEAD_DIM)] * 3,
        out_specs=(row(w["u"].shape[1]),
                   pl.BlockSpec((groups, n_qt, HEAD_DIM, NSA_HPG * Q_BLOCK), lambda i: (0, i, 0, 0)),
                   pl.BlockSpec((2, tm, LANES), lambda i: (0, i, 0)), row(LANES), row(LANES),
                   pl.BlockSpec((groups, vrows, tm), lambda i: (0, 0, i)),
                   pl.BlockSpec((groups, vrows, tm), lambda i: (0, 0, i)),
                   pl.BlockSpec((n_qt, groups, gate_rows, Q_BLOCK), lambda i: (i, 0, 0, 0)),
                   row(w["d"].shape[1])),
        compiler_params=_cparams(("parallel",)),
        name="in_proj",
    )(x, g, *weights, *tabs)


def _s5_prep_kernel(lr_ref, li_ref, ldt_ref, btr_ref, bti_ref, ctr_ref, cti_ref,
                    bre_ref, bim_ref, cre_ref, cim_ref, tab_ref):
    lr = lr_ref[...]
    li = li_ref[...]
    dt = jnp.exp(ldt_ref[...])
    mag = jnp.exp(lr * dt)
    ar = mag * jnp.cos(li * dt)
    ai = mag * jnp.sin(li * dt)
    nr = ar - 1.0
    ni = ai
    den = lr * lr + li * li
    f_re = (nr * lr + ni * li) / den
    f_im = (ni * lr - nr * li) / den
    btr = btr_ref[...]
    bti = bti_ref[...]
    rows = lax.broadcasted_iota(I32, btr.shape, 0) // S5_GROUP
    cols = lax.broadcasted_iota(I32, btr.shape, 1) // S5_STATE
    diag = rows == cols
    bre_ref[...] = jnp.where(diag, f_re * btr - f_im * bti, 0.0).astype(BF16)
    bim_ref[...] = jnp.where(diag, f_re * bti + f_im * btr, 0.0).astype(BF16)
    cre_ref[...] = jnp.where(diag, ctr_ref[...], 0.0).astype(BF16)
    cim_ref[...] = jnp.where(diag, cti_ref[...], 0.0).astype(BF16)
    pr = [ar]
    pi = [ai]
    for _ in range(SUBLANES - 1):
        pr_n = pr[-1] * ar - pi[-1] * ai
        pi_n = pr[-1] * ai + pi[-1] * ar
        pr.append(pr_n)
        pi.append(pi_n)
    rid = lax.broadcasted_iota(I32, (SUBLANES, lr.shape[1]), 0)
    zero = jnp.zeros((SUBLANES, lr.shape[1]), F32)
    p_re = zero
    p_im = zero
    for k in range(SUBLANES):
        p_re = jnp.where(rid == k, pr[k], p_re)
        p_im = jnp.where(rid == k, pi[k], p_im)
    for n, k in enumerate((1, 2, 4)):
        tab_ref[2 * n] = jnp.where(rid >= k, pr[k - 1], 0.0)
        tab_ref[2 * n + 1] = jnp.where(rid >= k, pi[k - 1], 0.0)
    tab_ref[6] = p_re
    tab_ref[7] = p_im


def _s5_prep(lam_re, lam_im, log_dt, b_re, b_im, c_re, c_im):
    g, p = lam_re.shape
    h = b_re.shape[-1]
    gp = g * p
    rowv = lambda a: a.reshape(1, gp)
    tile_t = lambda a: jnp.tile(a.reshape(h, gp), (g, 1))
    btr = tile_t(jnp.transpose(b_re, (2, 0, 1)))
    bti = tile_t(jnp.transpose(b_im, (2, 0, 1)))
    ctr = tile_t(jnp.transpose(c_re, (1, 0, 2)))
    cti = tile_t(jnp.transpose(c_im, (1, 0, 2)))
    mat = jax.ShapeDtypeStruct((g * h, gp), BF16)
    return pl.pallas_call(
        _s5_prep_kernel,
        out_shape=(mat, mat, mat, mat, jax.ShapeDtypeStruct((8, SUBLANES, gp), F32)),
        name="s5_prep",
    )(rowv(lam_re), rowv(lam_im), rowv(jnp.repeat(log_dt, p)), btr, bti, ctr, cti)


def _s5_kernel(u_ref, bre_ref, bim_ref, cre_ref, cim_ref, tab_ref, d_ref, wg_ref, bg_ref,
               y_ref, xr_ref, xi_ref, car_ref):
    @pl.when(pl.program_id(0) == 0)
    def _():
        car_ref[...] = jnp.zeros_like(car_ref)

    u = u_ref[...]
    ub = u.astype(BF16)
    xr_ref[...] = jnp.dot(ub, bre_ref[...], preferred_element_type=F32)
    xi_ref[...] = jnp.dot(ub, bim_ref[...], preferred_element_type=F32)
    tb = u.shape[0]

    def slab(r, carry):
        cr, ci = carry
        off = pl.multiple_of(r * SUBLANES, SUBLANES)
        xr = xr_ref[pl.ds(off, SUBLANES), :]
        xi = xi_ref[pl.ds(off, SUBLANES), :]
        for n, k in enumerate((1, 2, 4)):
            tr = tab_ref[2 * n]
            ti = tab_ref[2 * n + 1]
            sr = pltpu.roll(xr, k, 0)
            si = pltpu.roll(xi, k, 0)
            xr, xi = xr + tr * sr - ti * si, xi + tr * si + ti * sr
        pr = tab_ref[6]
        pi = tab_ref[7]
        xr, xi = xr + pr * cr - pi * ci, xi + pr * ci + pi * cr
        xr_ref[pl.ds(off, SUBLANES), :] = xr
        xi_ref[pl.ds(off, SUBLANES), :] = xi
        return xr[SUBLANES - 1:SUBLANES, :], xi[SUBLANES - 1:SUBLANES, :]

    cr, ci = lax.fori_loop(0, tb // SUBLANES, slab, (car_ref[0:1, :], car_ref[1:2, :]))
    car_ref[0:1, :] = cr
    car_ref[1:2, :] = ci
    y = _bdot_nt(xr_ref[...], cre_ref[...]) - _bdot_nt(xi_ref[...], cim_ref[...])
    y = y + d_ref[...] * u
    y = jax.nn.gelu(y)
    y = y * jax.nn.sigmoid(_bdot(y, wg_ref[...]) + bg_ref[...])
    y_ref[...] = y.astype(y_ref.dtype)


def _s5_mixer(u, prep, d_skip, w_glu, b_glu):
    seq, w = u.shape
    bre, bim, cre, cim, tab = prep
    gp = bre.shape[1]
    tb = min(seq, 512)
    full = lambda a: pl.BlockSpec(a.shape, lambda i: (0,) * a.ndim)
    return pl.pallas_call(
        _s5_kernel,
        out_shape=jax.ShapeDtypeStruct((seq, w), BF16),
        grid=(seq // tb,),
        in_specs=[pl.BlockSpec((tb, w), lambda i: (i, 0)), full(bre), full(bim), full(cre),
                  full(cim), full(tab), full(d_skip), full(w_glu), full(b_glu)],
        out_specs=pl.BlockSpec((tb, w), lambda i: (i, 0)),
        scratch_shapes=[pltpu.VMEM((tb, gp), F32), pltpu.VMEM((tb, gp), F32),
                        pltpu.VMEM((SUBLANES, gp), F32)],
        compiler_params=_cparams(("arbitrary",)),
        name="s5_mixer",
    )(u, bre, bim, cre, cim, tab, d_skip, w_glu, b_glu)


def _compress_kernel(x_ref, w1g_ref, w1_ref, w2k_ref, w2v_ref, pe_ref, kc_ref, vct_ref):
    seq = x_ref.shape[1]
    nc = seq // CMP_STRIDE
    groups = w1g_ref.shape[1]
    fdot = lambda a, b: jnp.dot(a, b, preferred_element_type=F32)

    def hidden(kind):
        taps = [x_ref[kind, pl.ds(l, nc, stride=CMP_STRIDE), :].astype(BF16) for l in range(CMP_STRIDE)]
        pew = _bdot(pe_ref[kind], w1_ref[kind])[0:1, :]
        out = []
        for g in range(groups):
            a = fdot(taps[0], w1g_ref[kind, g, 0])
            b = fdot(taps[0], w1g_ref[kind, g, CMP_STRIDE])
            for l in range(1, CMP_STRIDE):
                a = a + fdot(taps[l], w1g_ref[kind, g, l])
                b = b + fdot(taps[l], w1g_ref[kind, g, CMP_STRIDE + l])
            out.append(jax.nn.gelu(a + pltpu.roll(b, nc - 1, 0) + pew).astype(BF16))
        return out

    hk = hidden(0)
    kc = fdot(hk[0], w2k_ref[0])
    for g in range(1, groups):
        kc = kc + fdot(hk[g], w2k_ref[g])
    kc_ref[...] = kc.astype(kc_ref.dtype)

    hv = hidden(1)
    dh = w2v_ref.shape[0]
    rid = lax.broadcasted_iota(I32, (vct_ref.shape[1] - dh, nc), 0)
    tail = jnp.where(rid == 0, 1.0, 0.0).astype(vct_ref.dtype)
    for g in range(groups):
        vct_ref[g, 0:dh, :] = lax.dot_general(w2v_ref[...], hv[g], (((1,), (1,)), ((), ())),
                                              preferred_element_type=F32).astype(vct_ref.dtype)
        vct_ref[g, dh:, :] = tail


def _compress(x, w1g, w1, w2k, w2v, pe):
    seq = x.shape[1]
    nc = seq // CMP_STRIDE
    groups = w1g.shape[1]
    dh = w2v.shape[0]
    return pl.pallas_call(
        _compress_kernel,
        out_shape=(jax.ShapeDtypeStruct((nc, groups * dh), BF16),
                   jax.ShapeDtypeStruct((groups, dh + BF16_SUBLANES, nc), BF16)),
        compiler_params=pltpu.CompilerParams(vmem_limit_bytes=VMEM_LIMIT),
        name="compress",
    )(x, w1g, w1, w2k, w2v, pe)


def _nsa_pair_kernel(qt_ref, kc_ref, vct_ref, ks_ref, vst_ref, kw_ref, vwt_ref, g_ref,
                     out_ref, sc_ref, imp_ref, selb_ref, sa_ref, sb_ref, *, nc, ns):
    qi = pl.program_id(0)
    groups = range(qt_ref.shape[0])
    dh, width = qt_ref.shape[2:]
    zeros = jnp.zeros((dh, width), qt_ref.dtype)
    qts = [jnp.concatenate([qt_ref[g, 0] if k == g else zeros for k in groups], axis=0) for g in groups]
    vrows = vst_ref.shape[1]
    lane = lax.broadcasted_iota(I32, (1, width), 1)
    t_row = qi * Q_BLOCK + (lane % Q_BLOCK)
    tq = qi * Q_BLOCK + lax.broadcasted_iota(I32, (1, Q_BLOCK), 1)
    all_heads = lambda a: jnp.concatenate([a] * NSA_HPG, axis=1)
    fdot = lambda a, b: jnp.dot(a, b, preferred_element_type=F32)

    pad = SUBLANES
    ch = min(CMP_CHUNK, nc)
    n_ch = (((qi + 1) * Q_BLOCK - CMP_BLOCK) // CMP_STRIDE) // ch + 1

    def cmp_scores(c, ms):
        off = pl.multiple_of(c * ch, ch)
        ci = off + lax.broadcasted_iota(I32, (ch, 1), 0)
        vis = ci * CMP_STRIDE + (CMP_BLOCK - 1) <= t_row
        out = []
        for g in groups:
            s = jnp.where(vis, fdot(kc_ref[pl.ds(off, ch), :], qts[g]), NEG_INF)
            sc_ref[g, pl.ds(off, ch), :] = s
            out.append(jnp.maximum(ms[g], jnp.max(s, axis=0, keepdims=True)))
        return tuple(out)

    ms = lax.fori_loop(0, n_ch, cmp_scores, tuple(jnp.full((1, width), NEG_INF, F32) for _ in groups))
    ms = [jnp.maximum(m, 0.1 * NEG_INF) for m in ms]

    def cmp_probs(c, accs):
        off = pl.multiple_of(c * ch, ch)
        out = []
        for g in groups:
            e = jnp.exp2(sc_ref[g, pl.ds(off, ch), :] - ms[g])
            sc_ref[g, pl.ds(off, ch), :] = e
            out.append(accs[g] + fdot(vct_ref[g, :, pl.ds(off, ch)], e.astype(BF16)))
        return tuple(out)

    accs = lax.fori_loop(0, n_ch, cmp_probs, tuple(jnp.zeros((vrows, width), F32) for _ in groups))
    inv_l = [1.0 / jnp.maximum(a[dh:dh + 1, :], 1e-30) for a in accs]
    o_cmp = [a[0:dh, :] * i for a, i in zip(accs, inv_l)]
    imp_ref[...] = jnp.zeros_like(imp_ref)

    def cmp_importance(c, _):
        off = pl.multiple_of(c * ch, ch)
        for g in groups:
            p = sc_ref[g, pl.ds(off, ch), :] * inv_l[g]
            imp = p[:, 0:Q_BLOCK]
            for h in range(1, NSA_HPG):
                imp = imp + p[:, h * Q_BLOCK:(h + 1) * Q_BLOCK]
            imp_ref[g, pl.ds(pl.multiple_of(pad + off, pad), ch), :] = imp
        return 0

    lax.fori_loop(0, n_ch, cmp_importance, 0)

    span_w = WINDOW + Q_BLOCK
    woff = pl.multiple_of(jnp.maximum(qi * Q_BLOCK - WINDOW, 0), Q_BLOCK)
    dpos = tq - (woff + lax.broadcasted_iota(I32, (span_w, 1), 0))
    ok = (dpos >= 0) & (dpos < WINDOW)
    win_bias = all_heads(jnp.where(ok, 0.0, NEG_INF))
    o_win = []
    for g in groups:
        sa_ref[g] = fdot(ks_ref[0:KV_TILE, :], qts[g])
        s = fdot(kw_ref[pl.ds(woff, span_w), :], qts[g]) + win_bias
        p = jnp.exp2(s - jnp.max(s, axis=0, keepdims=True)).astype(BF16)
        acc = fdot(vwt_ref[g, :, pl.ds(woff, span_w)], p)
        o_win.append(acc[0:dh, :] / jnp.maximum(acc[dh:dh + 1, :], 1e-30))

    ratio = SLC_BLOCK // CMP_STRIDE
    span = ratio + CMP_BLOCK // CMP_STRIDE - 1
    lead = CMP_BLOCK // CMP_STRIDE - 1
    jb = lax.broadcasted_iota(I32, (ns, 1), 0)
    cur = tq // SLC_BLOCK
    bonus = ((jb == 0) | (jb == cur) | (jb == cur - 1)).astype(F32) * FORCE_BONUS
    jf = jb.astype(F32)
    for g in groups:
        imp_slc = imp_ref[g, pl.ds(pad - lead, ns, stride=ratio), :]
        for sft in range(1, span):
            imp_slc = imp_slc + imp_ref[g, pl.ds(pad - lead + sft, ns, stride=ratio), :]
        score = jnp.where(jb * SLC_BLOCK <= tq, imp_slc + bonus, NEG_INF)
        for _ in range(min(SLC_TOPK, ns)):
            mx = jnp.max(score, axis=0, keepdims=True)
            first = jnp.min(jnp.where(score == mx, jf, float(ns)), axis=0, keepdims=True)
            score = jnp.where(jf == first, -jnp.inf, score)
        selb_ref[g] = jnp.where(score == -jnp.inf, 0.0, NEG_INF)

    blocks = KV_TILE // SLC_BLOCK

    def causal(kt, visible):
        kpos = kt * KV_TILE + lax.broadcasted_iota(I32, (KV_TILE, 1), 0)
        return jnp.where(kpos <= tq, visible, NEG_INF)

    def tile_bias(g, kt, is_causal):
        bias_t = selb_ref[g, pl.ds(pl.multiple_of(kt * blocks, blocks), blocks), :]
        bias = jnp.concatenate(
            [jnp.broadcast_to(bias_t[b:b + 1, :], (SLC_BLOCK, Q_BLOCK)) for b in range(blocks)], axis=0)
        return all_heads(causal(kt, bias) if is_causal else bias)

    def scores(kt, dst_ref, is_causal):
        off = pl.multiple_of(kt * KV_TILE, KV_TILE)
        for g in groups:
            dst_ref[g] = fdot(ks_ref[pl.ds(off, KV_TILE), :], qts[g]) + tile_bias(g, kt, is_causal)

    def consume(kt, src_ref, carry, extra_bias=None):
        off = pl.multiple_of(kt * KV_TILE, KV_TILE)
        out = []
        for g in groups:
            m, acc = carry[g]
            s = src_ref[g]
            if extra_bias is not None:
                s = s + extra_bias
            m_new = jnp.maximum(m, jnp.max(s, axis=0, keepdims=True))
            p = jnp.exp2(s - m_new).astype(BF16)
            out.append((m_new, jnp.exp2(m - m_new) * acc + fdot(vst_ref[g, :, pl.ds(off, KV_TILE)], p)))
        return tuple(out)

    n_before = (qi * Q_BLOCK) // KV_TILE
    for g in groups:
        sa_ref[g] = sa_ref[g] + tile_bias(g, 0, False)

    def tile_pair(i, carry):
        scores(2 * i + 1, sb_ref, False)
        carry = consume(2 * i, sa_ref, carry)
        scores(2 * i + 2, sa_ref, False)
        return consume(2 * i + 1, sb_ref, carry)

    init = tuple((jnp.full((1, width), NEG_INF, F32), jnp.zeros((vrows, width), F32)) for _ in groups)
    carry = lax.fori_loop(0, n_before // 2, tile_pair, init)
    k0 = 2 * (n_before // 2)
    scores(k0 + 1, sb_ref, True)
    carry = consume(k0, sa_ref, carry, all_heads(causal(k0, 0.0)))
    carry = consume(k0 + 1, sb_ref, carry)

    for g in groups:
        acc = carry[g][1]
        o_slc = acc[0:dh, :] / jnp.maximum(acc[dh:dh + 1, :], 1e-30)
        gate = jax.nn.sigmoid(g_ref[0, g])
        for h in range(NSA_HPG):
            hs = slice(h * Q_BLOCK, (h + 1) * Q_BLOCK)
            y = (gate[3 * h:3 * h + 1, :] * o_cmp[g][:, hs] + gate[3 * h + 1:3 * h + 2, :] * o_slc[:, hs]
                 + gate[3 * h + 2:3 * h + 3, :] * o_win[g][:, hs])
            row = (g * NSA_HPG + h) * dh
            out_ref[row:row + dh, :] = y.astype(out_ref.dtype)


def _nsa_pair(qt, kc, vct, ks, vst, kw, vwt, gates):
    g, nqt, dh, width = qt.shape
    seq = ks.shape[0]
    nc = kc.shape[0]
    ns = seq // SLC_BLOCK
    resident = lambda a: pl.BlockSpec(a.shape, lambda qi: (0,) * a.ndim, pipeline_mode=pl.Buffered(1))
    return pl.pallas_call(
        functools.partial(_nsa_pair_kernel, nc=nc, ns=ns),
        out_shape=jax.ShapeDtypeStruct((g * NSA_HPG * dh, seq), BF16),
        grid=(nqt,),
        in_specs=[pl.BlockSpec((g, 1, dh, width), lambda qi: (0, qi, 0, 0)),
                  resident(kc), resident(vct), resident(ks), resident(vst), resident(kw), resident(vwt),
                  pl.BlockSpec((1,) + gates.shape[1:], lambda qi: (qi, 0, 0, 0))],
        out_specs=pl.BlockSpec((g * NSA_HPG * dh, Q_BLOCK), lambda qi: (0, qi)),
        scratch_shapes=[pltpu.VMEM((g, nc, width), F32),
                        pltpu.VMEM((g, nc + 2 * SUBLANES, Q_BLOCK), F32),
                        pltpu.VMEM((g, ns, Q_BLOCK), F32),
                        pltpu.VMEM((g, KV_TILE, width), F32),
                        pltpu.VMEM((g, KV_TILE, width), F32)],
        compiler_params=_cparams(("arbitrary",)),
        name="sparse_attention",
    )(qt, kc, vct, ks, vst, kw, vwt, gates)


def _split_bf16(x):
    hi = x.astype(BF16)
    return hi, (x - hi.astype(F32)).astype(BF16)


def _gla_kernel(h_ref, wa_ref, ba_ref, ng_ref, y_ref, st_ref):
    @pl.when(pl.program_id(0) == 0)
    def _():
        st_ref[...] = jnp.zeros_like(st_ref)

    kw = GLA_HEADS * GLA_DK
    vw = GLA_HEADS * GLA_DV
    c = GLA_CHUNK
    sb = min(GLA_SUPER, h_ref.shape[0])
    ri = lax.broadcasted_iota(I32, (sb, sb), 0)
    cj = lax.broadcasted_iota(I32, (sb, sb), 1)
    same_chunk = (ri // c) == (cj // c)
    tril = same_chunk & (ri >= cj)
    tril_m = jnp.where(tril, 1.0, 0.0).astype(BF16)
    chunk_m = jnp.where(same_chunk, 1.0, 0.0).astype(BF16)
    ones_c = jnp.ones((c, LANES), BF16)
    krow = lax.broadcasted_iota(I32, (kw, vw), 0) // GLA_DK
    vcol = lax.broadcasted_iota(I32, (kw, vw), 1) // GLA_DV
    blockdiag = krow == vcol
    klane = lax.broadcasted_iota(I32, (1, kw), 1) // GLA_DK
    vlane = lax.broadcasted_iota(I32, (1, vw), 1) // GLA_DV
    va = lax.broadcasted_iota(I32, (vw, vw), 0) // GLA_DV
    vb = lax.broadcasted_iota(I32, (vw, vw), 1) // GLA_DV
    head_avg = jnp.where(va == vb, 1.0 / GLA_DV, 0.0).astype(BF16)
    tn = (((0,), (0,)), ((), ()))

    state = st_ref[...]
    for blk in range(h_ref.shape[0] // sb):
        rows = slice(blk * sb, (blk + 1) * sb)
        q = h_ref[rows, 0:kw] * (GLA_DK ** -0.5)
        k = h_ref[rows, kw:2 * kw]
        v = h_ref[rows, 2 * kw:2 * kw + vw].astype(BF16)
        r = h_ref[rows, 2 * kw + vw:2 * kw + 2 * vw]
        a_low = h_ref[rows, 2 * kw + 2 * vw:2 * kw + 2 * vw + LANES]
        la = jax.nn.log_sigmoid(_hdot(a_low, wa_ref[...]) + ba_ref[...]) / GLA_TAU
        la_hi, la_lo = _split_bf16(la)
        sum_la = lambda mat: (jnp.dot(mat, la_hi, preferred_element_type=F32)
                              + jnp.dot(mat, la_lo, preferred_element_type=F32))
        bcum = sum_la(tril_m)
        b_last = sum_la(chunk_m)
        q_t = (q * jnp.exp(bcum)).astype(BF16)
        k_t = (k * jnp.exp(-bcum)).astype(BF16)
        k_d = (k * jnp.exp(b_last - bcum)).astype(BF16)
        o = jnp.zeros((sb, vw), F32)
        for hd in range(GLA_HEADS):
            attn = _bdot_nt(jnp.where(klane == hd, q_t, jnp.zeros_like(q_t)), k_t)
            attn = jnp.where(tril, attn, 0.0).astype(BF16)
            o = o + jnp.where(vlane == hd, jnp.dot(attn, v, preferred_element_type=F32), 0.0)
        inter = []
        for n in range(sb // c):
            cs = slice(n * c, (n + 1) * c)
            inter.append(jnp.dot(q_t[cs], state.astype(BF16), preferred_element_type=F32))
            kv = jnp.where(blockdiag, lax.dot_general(k_d[cs], v[cs], tn, preferred_element_type=F32), 0.0)
            tot = (lax.dot_general(la_hi[cs], ones_c, tn, preferred_element_type=F32)
                   + lax.dot_general(la_lo[cs], ones_c, tn, preferred_element_type=F32))
            decay = jnp.exp(tot)
            state = jnp.concatenate([decay] * (vw // LANES), axis=1) * state + kv
        o = o + jnp.concatenate(inter, axis=0)
        sq_hi, sq_lo = _split_bf16(o * o)
        ms = (jnp.dot(sq_hi, head_avg, preferred_element_type=F32)
              + jnp.dot(sq_lo, head_avg, preferred_element_type=F32))
        o = o * lax.rsqrt(ms + EPS)
        y_ref[rows, :] = (o * ng_ref[...] * (r * jax.nn.sigmoid(r))).astype(y_ref.dtype)
    st_ref[...] = state


def _gla_mixer(hd, w_a2p, b_a, norm_g):
    seq, wid = hd.shape
    tb = min(seq, 512)
    kw = GLA_HEADS * GLA_DK
    vw = GLA_HEADS * GLA_DV
    full = lambda a: pl.BlockSpec(a.shape, lambda i: (0, 0))
    return pl.pallas_call(
        _gla_kernel,
        out_shape=jax.ShapeDtypeStruct((seq, vw), BF16),
        grid=(seq // tb,),
        in_specs=[pl.BlockSpec((tb, wid), lambda i: (i, 0)), full(w_a2p), full(b_a), full(norm_g)],
        out_specs=pl.BlockSpec((tb, vw), lambda i: (i, 0)),
        scratch_shapes=[pltpu.VMEM((kw, vw), F32)],
        compiler_params=_cparams(("arbitrary",)),
        name="gla_mixer",
    )(hd, w_a2p, b_a, norm_g)


def _merge_kernel(x_ref, g_ref, ys_ref, yn_ref, yg_ref, wgm_ref, ws_ref, wn_ref, wg_ref, wo_ref, o_ref):
    x = x_ref[...]
    d = x.shape[1]
    xn = _rms(x, g_ref[...]).astype(BF16)
    gm = jax.nn.sigmoid(jnp.dot(xn, wgm_ref[...], preferred_element_type=F32))
    mixed = (gm[:, 0:d] * jnp.dot(ys_ref[...], ws_ref[...], preferred_element_type=F32)
             + gm[:, d:2 * d] * lax.dot_general(yn_ref[...], wn_ref[...], (((0,), (0,)), ((), ())),
                                                preferred_element_type=F32)
             + gm[:, 2 * d:3 * d] * jnp.dot(yg_ref[...], wg_ref[...], preferred_element_type=F32))
    o_ref[...] = x + _bdot(mixed, wo_ref[...])


def _merge(x, g, ys, yn, yg, wgm, ws, wn, wg, wo):
    seq, d = x.shape
    tm = min(seq, 512)
    row = lambda a: pl.BlockSpec((tm, a.shape[1]), lambda i: (i, 0))
    full = lambda a: pl.BlockSpec(a.shape, lambda i: (0, 0))
    return pl.pallas_call(
        _merge_kernel,
        out_shape=jax.ShapeDtypeStruct((seq, d), F32),
        grid=(seq // tm,),
        in_specs=[row(x), full(g), row(ys), pl.BlockSpec((yn.shape[0], tm), lambda i: (0, i)), row(yg),
                  full(wgm), full(ws), full(wn), full(wg), full(wo)],
        out_specs=row(x),
        compiler_params=_cparams(("parallel",)),
        name="merge",
    )(x, g, ys, yn, yg, wgm, ws, wn, wg, wo)


def _ffn_kernel(x_ref, xp_ref, g_ref, wug_ref, wuv_ref, cwg_ref, cwv_ref, cbg_ref, cbv_ref, wd_ref,
                fg_ref, o_ref, xn_ref, *, final):
    i = pl.program_id(0)
    j = pl.program_id(1)
    tm = x_ref.shape[0]
    halo = xp_ref.shape[0]

    @pl.when(j == 0)
    def _():
        xn_ref[0:halo, :] = jnp.where(i == 0, 0.0, _rms(xp_ref[...], g_ref[...])).astype(BF16)
        xn_ref[halo:halo + tm, :] = _rms(x_ref[...], g_ref[...]).astype(BF16)

    xn = xn_ref[...]

    def conv(w_ref, cw_ref, cb_ref):
        h = jnp.dot(xn, w_ref[...], preferred_element_type=F32)
        hc = cb_ref[...]
        for t in range(CONV_WIDTH):
            sh = CONV_WIDTH - 1 - t
            hs = h if sh == 0 else pltpu.roll(h, sh, 0)
            hc = hc + cw_ref[t:t + 1, :] * hs[halo:halo + tm, :]
        return hc

    act = jax.nn.gelu(conv(wug_ref, cwg_ref, cbg_ref)) * conv(wuv_ref, cwv_ref, cbv_ref)
    part = _bdot(act, wd_ref[...])

    @pl.when(j == 0)
    def _():
        o_ref[...] = x_ref[...] + part

    @pl.when(j > 0)
    def _():
        o_ref[...] = o_ref[...] + part

    if final:
        @pl.when(j == pl.num_programs(1) - 1)
        def _():
            o_ref[...] = _rms(o_ref[...], fg_ref[...])


def _ffn(x, g, w_up, conv_w, conv_b, w_down, final_g, final):
    seq, d = x.shape
    dff = w_down.shape[0]
    tm = min(seq, 1024)
    nj = 2
    tn = dff // nj
    halo = SUBLANES
    hb = tm // halo
    return pl.pallas_call(
        functools.partial(_ffn_kernel, final=final),
        out_shape=jax.ShapeDtypeStruct((seq, d), F32),
        grid=(seq // tm, nj),
        in_specs=[pl.BlockSpec((tm, d), lambda i, j: (i, 0)),
                  pl.BlockSpec((halo, d), lambda i, j: (jnp.maximum(i * hb - 1, 0), 0)),
                  pl.BlockSpec((1, d), lambda i, j: (0, 0)),
                  pl.BlockSpec((d, tn), lambda i, j: (0, j)),
                  pl.BlockSpec((d, tn), lambda i, j: (0, nj + j)),
                  pl.BlockSpec((CONV_WIDTH, tn), lambda i, j: (0, j)),
                  pl.BlockSpec((CONV_WIDTH, tn), lambda i, j: (0, nj + j)),
                  pl.BlockSpec((1, tn), lambda i, j: (0, j)),
                  pl.BlockSpec((1, tn), lambda i, j: (0, nj + j)),
                  pl.BlockSpec((tn, d), lambda i, j: (j, 0)),
                  pl.BlockSpec((1, d), lambda i, j: (0, 0))],
        out_specs=pl.BlockSpec((tm, d), lambda i, j: (i, 0)),
        scratch_shapes=[pltpu.VMEM((halo + tm, d), BF16)],
        compiler_params=_cparams(("parallel", "arbitrary")),
        name="conv_ffn",
    )(x, x, g, w_up, w_up, conv_w, conv_w, conv_b, conv_b, w_down, final_g)


def _split_w_in(w):
    d = w.shape[0]
    s5w = d // 4
    nsa_w = NSA_HEADS * HEAD_DIM
    kvw = NSA_KV_GROUPS * HEAD_DIM
    kw = GLA_HEADS * GLA_DK
    vw = GLA_HEADS * GLA_DV
    sizes = (s5w, nsa_w, kvw, kvw, kvw, kvw, kvw, kvw, 3 * NSA_HEADS, kw, kw, vw, 16, vw, 3 * d)
    assert sum(sizes) == w.shape[1]
    offs = [0]
    for sz in sizes:
        offs.append(offs[-1] + sz)
    col = lambda n: w[:, offs[n]:offs[n + 1]]
    (u, q, kc, vc, ks, vs, kwn, vwn, gn, gq, gk, gv, ga, gr, gm) = [col(n) for n in range(len(sizes))]
    padl = lambda a: jnp.pad(a, ((0, 0), (0, LANES - a.shape[1])))
    per_group = 3 * NSA_HPG
    gates_t = jnp.pad(gn.T.reshape(NSA_KV_GROUPS, per_group, d),
                      ((0, 0), (0, BF16_SUBLANES - per_group), (0, 0))).reshape(-1, d)
    out = dict(u=u, k=jnp.concatenate([kc, ks, kwn], axis=1), vc=vc,
               d=jnp.concatenate([gq, gk, gv, gr, padl(ga)], axis=1),
               qt=q.T, vt=jnp.concatenate([vs, vwn], axis=1).T, gt=gates_t, gm=gm)
    return {name: a.astype(BF16) for name, a in out.items()}


def kernel(x, positions, norm1_g, w_in, s5_lam_re, s5_lam_im, s5_log_dt, s5_b_re, s5_b_im, s5_c_re, s5_c_im, s5_d, s5_w_glu, s5_b_glu, nsa_pe_k, nsa_pe_v, nsa_ck_w1, nsa_ck_w2, nsa_cv_w1, nsa_cv_w2, gla_w_a2, gla_b_a, gla_norm_g, w_br_s5, w_br_nsa, w_br_gla, w_out, norm2_g, ffn_w_up, ffn_conv_w, ffn_conv_b, ffn_w_down, final_g):
    bsz, seq, d = x.shape
    depth = w_in.shape[0]
    assert bsz == 1 and seq % KV_TILE == 0 and d % LANES == 0
    g = NSA_KV_GROUPS
    dh = HEAD_DIM
    row = lambda a: a.reshape(1, -1)

    xs = x.reshape(seq, d)
    tabs = _rope_tables(positions.reshape(seq))

    for l in range(depth):
        w = _split_w_in(w_in[l])
        wgm = w["gm"]
        g1 = row(norm1_g[l])
        u_s5, qt, x_cmp, ks, kwn, vst, vwt, gates, hd = _project(xs, g1, w, tabs)

        prep = _s5_prep(s5_lam_re[l], s5_lam_im[l], s5_log_dt[l], s5_b_re[l], s5_b_im[l],
                        s5_c_re[l], s5_c_im[l])
        y_s5 = _s5_mixer(u_s5, prep, row(s5_d[l]), s5_w_glu[l].astype(BF16), row(s5_b_glu[l]))

        pe = jnp.stack([nsa_pe_k[l].reshape(1, -1), nsa_pe_v[l].reshape(1, -1)])
        pe = jnp.broadcast_to(pe, (2, SUBLANES, pe.shape[-1]))
        w1 = jnp.stack([nsa_ck_w1[l], nsa_cv_w1[l]]).astype(BF16)
        taps = w1.reshape(2, 1, CMP_BLOCK, dh, w1.shape[-1])
        w1g = jnp.concatenate(
            [jnp.pad(taps, ((0, 0), (0, 0), (0, 0), (gi * dh, (g - 1 - gi) * dh), (0, 0))) for gi in range(g)],
            axis=1)
        w2k = jnp.stack([jnp.pad(nsa_ck_w2[l], ((0, 0), (gi * dh, (g - 1 - gi) * dh))) for gi in range(g)])
        kc, vct = _compress(x_cmp, w1g, w1, w2k.astype(BF16), nsa_cv_w2[l].T.astype(BF16), pe)
        y_nsa = _nsa_pair(qt, kc, vct, ks, vst, kwn, vwt, gates)

        w_a2p = jnp.pad(gla_w_a2[l], ((0, LANES - gla_w_a2.shape[1]), (0, 0)))
        y_gla = _gla_mixer(hd, w_a2p, row(gla_b_a[l]), row(gla_norm_g[l]))

        xs = _merge(xs, g1, y_s5, y_nsa, y_gla, wgm, w_br_s5[l].astype(BF16), w_br_nsa[l].astype(BF16),
                    w_br_gla[l].astype(BF16), w_out[l].astype(BF16))
        xs = _ffn(xs, row(norm2_g[l]), ffn_w_up[l].astype(BF16), ffn_conv_w[l], row(ffn_conv_b[l]),
                  ffn_w_down[l].astype(BF16), row(final_g), final=(l == depth - 1))
    return xs.reshape(bsz, seq, d)
```

```python
import functools
import math

import jax
import jax.numpy as jnp
from jax import lax
from jax.experimental import pallas as pl
from jax.experimental.pallas import tpu as pltpu

F32 = jnp.float32
BF16 = jnp.bfloat16
I32 = jnp.int32

EPS = 1e-6
NEG_INF = -1e30

S5_GROUP = 16
S5_STATE = 64
NSA_HEADS = 8
NSA_KV_GROUPS = 2
NSA_HPG = NSA_HEADS // NSA_KV_GROUPS
HEAD_DIM = 64
ROPE_DIM = 16
ROPE_THETA = 500000.0
CMP_BLOCK = 32
CMP_STRIDE = 16
SLC_BLOCK = 64
SLC_TOPK = 16
WINDOW = 512
Q_BLOCK = 128
FORCE_BONUS = 1.0e4
GLA_HEADS = 4
GLA_DK = 32
GLA_DV = 64
GLA_TAU = 16.0
GLA_CHUNK = 64
GLA_SUPER = 256
CONV_WIDTH = 3

LANES = 128
SUBLANES = 8
KV_TILE = 512
CMP_CHUNK = 256
BF16_SUBLANES = 16
VMEM_LIMIT = 56 * 1024 * 1024


def _cparams(sem):
    return pltpu.CompilerParams(dimension_semantics=sem, vmem_limit_bytes=VMEM_LIMIT)


def _bdot(a, b):
    return jnp.dot(a.astype(BF16), b.astype(BF16), preferred_element_type=F32)


def _bdot_nt(a, b):
    return lax.dot_general(a.astype(BF16), b.astype(BF16), (((1,), (1,)), ((), ())),
                           preferred_element_type=F32)


def _hdot(a, b):
    return jnp.dot(a, b, preferred_element_type=F32, precision=lax.Precision.HIGHEST)


def _rms(x, g):
    return x * lax.rsqrt(jnp.mean(x * x, axis=-1, keepdims=True) + EPS) * g


def _rope_tab_kernel(pos_ref, post_ref, invf_ref, invft_ref, c_ref, s1_ref, s2_ref, ct_ref, s1t_ref, s2t_ref):
    half = ROPE_DIM // 2

    def tables(ang, dim, c_out, s1_out, s2_out):
        cosv = jnp.cos(ang)
        sinv = jnp.sin(ang)
        c_out[...] = jnp.where(dim < ROPE_DIM, cosv, 1.0)
        s1_out[...] = jnp.where(dim < half, 0.0, jnp.where(dim < ROPE_DIM, sinv, 0.0))
        s2_out[...] = jnp.where(dim < half, -sinv, 0.0)

    ang = pos_ref[...].astype(F32) * invf_ref[...]
    tables(ang, lax.broadcasted_iota(I32, ang.shape, 1) % HEAD_DIM, c_ref, s1_ref, s2_ref)
    ang_t = invft_ref[...] * post_ref[...].astype(F32)
    tables(ang_t, lax.broadcasted_iota(I32, ang_t.shape, 0), ct_ref, s1t_ref, s2t_ref)


def _rope_tables(positions):
    seq = positions.shape[0]
    tm = min(seq, 2048)
    inv_freq = ROPE_THETA ** (-jnp.arange(0, ROPE_DIM, 2, dtype=F32) / ROPE_DIM)
    dim = jnp.arange(LANES) % HEAD_DIM
    invf = jnp.where(dim < ROPE_DIM, inv_freq[dim % (ROPE_DIM // 2)], 0.0)
    tab = jax.ShapeDtypeStruct((seq, LANES), F32)
    tab_t = jax.ShapeDtypeStruct((HEAD_DIM, seq), F32)
    return pl.pallas_call(
        _rope_tab_kernel,
        out_shape=(tab, tab, tab, tab_t, tab_t, tab_t),
        grid=(seq // tm,),
        in_specs=[pl.BlockSpec((tm, 1), lambda i: (i, 0)),
                  pl.BlockSpec((1, tm), lambda i: (0, i)),
                  pl.BlockSpec((1, LANES), lambda i: (0, 0)),
                  pl.BlockSpec((HEAD_DIM, 1), lambda i: (0, 0))],
        out_specs=tuple([pl.BlockSpec((tm, LANES), lambda i: (i, 0))] * 3
                        + [pl.BlockSpec((HEAD_DIM, tm), lambda i: (0, i))] * 3),
        compiler_params=_cparams(("parallel",)),
        name="rope_tables",
    )(positions.reshape(seq, 1), positions.reshape(1, seq), invf.reshape(1, LANES),
      invf[:HEAD_DIM].reshape(HEAD_DIM, 1))


def _proj_kernel(x_ref, g_ref, wu_ref, wk_ref, wvc_ref, wd_ref, wqt_ref, wvt_ref, wgt_ref,
                 c_ref, s1_ref, s2_ref, ct_ref, s1t_ref, s2t_ref,
                 u_ref, qt_ref, cmp_ref, ks_ref, kw_ref, vst_ref, vwt_ref, gt_ref, hd_ref):
    xn = _rms(x_ref[...], g_ref[...]).astype(BF16)
    tm = xn.shape[0]
    n_qt = tm // Q_BLOCK
    half = ROPE_DIM // 2
    fdot = lambda a, b: jnp.dot(a, b, preferred_element_type=F32)
    tdot = lambda w, a: lax.dot_general(w, a, (((1,), (1,)), ((), ())), preferred_element_type=F32)

    u_ref[...] = fdot(xn, wu_ref[...])
    hd_ref[...] = fdot(xn, wd_ref[...])
    cmp_ref[1] = fdot(xn, wvc_ref[...])
    hk = fdot(xn, wk_ref[...])
    c, s1, s2 = c_ref[...], s1_ref[...], s2_ref[...]
    for j, out in enumerate((cmp_ref.at[0], ks_ref, kw_ref)):
        piece = hk[:, j * LANES:(j + 1) * LANES]
        rot = piece * c + pltpu.roll(piece, half, 1) * s1 + pltpu.roll(piece, LANES - half, 1) * s2
        out[...] = rot.astype(out.dtype)

    q_t = tdot(wqt_ref[...], xn)
    ct, s1t, s2t = ct_ref[...], s1t_ref[...], s2t_ref[...]
    dh = ct.shape[0]
    for gh in range(q_t.shape[0] // dh):
        blk = q_t[gh * dh:(gh + 1) * dh, :]
        rot = blk * ct + pltpu.roll(blk, half, 0) * s1t + pltpu.roll(blk, dh - half, 0) * s2t
        rot = (rot * (dh ** -0.5 * math.log2(math.e))).astype(qt_ref.dtype)
        g, h = divmod(gh, NSA_HPG)
        for j in range(n_qt):
            qt_ref[g, j, :, h * Q_BLOCK:(h + 1) * Q_BLOCK] = rot[:, j * Q_BLOCK:(j + 1) * Q_BLOCK]

    v_t = tdot(wvt_ref[...], xn)
    rid = lax.broadcasted_iota(I32, (vst_ref.shape[1] - dh, tm), 0)
    tail = jnp.where(rid == 0, 1.0, 0.0).astype(vst_ref.dtype)
    for b, out in enumerate((vst_ref, vwt_ref)):
        for g in range(out.shape[0]):
            row = (b * out.shape[0] + g) * dh
            out[g, 0:dh, :] = v_t[row:row + dh, :].astype(out.dtype)
            out[g, dh:, :] = tail

    g_t = tdot(wgt_ref[...], xn)
    rows = gt_ref.shape[2]
    for j in range(n_qt):
        for g in range(gt_ref.shape[1]):
            gt_ref[j, g] = g_t[g * rows:(g + 1) * rows, j * Q_BLOCK:(j + 1) * Q_BLOCK]


def _project(x, g, w, tabs):
    seq, d = x.shape
    tm = min(seq, 512)
    n_qt = tm // Q_BLOCK
    groups = NSA_KV_GROUPS
    vrows = HEAD_DIM + BF16_SUBLANES
    gate_rows = w["gt"].shape[0] // groups
    row = lambda wid: pl.BlockSpec((tm, wid), lambda i: (i, 0))
    col = lambda r: pl.BlockSpec((r, tm), lambda i: (0, i))
    full = lambda a: pl.BlockSpec(a.shape, lambda i: (0, 0))
    weights = [w[k] for k in ("u", "k", "vc", "d", "qt", "vt", "gt")]
    sds = jax.ShapeDtypeStruct
    return pl.pallas_call(
        _proj_kernel,
        out_shape=(sds((seq, w["u"].shape[1]), F32),
                   sds((groups, seq // Q_BLOCK, HEAD_DIM, NSA_HPG * Q_BLOCK), BF16),
                   sds((2, seq, LANES), F32),
                   sds((seq, LANES), BF16), sds((seq, LANES), BF16),
                   sds((groups, vrows, seq), BF16), sds((groups, vrows, seq), BF16),
                   sds((seq // Q_BLOCK, groups, gate_rows, Q_BLOCK), F32),
                   sds((seq, w["d"].shape[1]), F32)),
        grid=(seq // tm,),
        in_specs=[row(d), full(g)] + [full(a) for a in weights]
                 + [row(LANES)] * 3 + [col(HEAD_DIM)] * 3,
        out_specs=(row(w["u"].shape[1]),
                   pl.BlockSpec((groups, n_qt, HEAD_DIM, NSA_HPG * Q_BLOCK), lambda i: (0, i, 0, 0)),
                   pl.BlockSpec((2, tm, LANES), lambda i: (0, i, 0)), row(LANES), row(LANES),
                   pl.BlockSpec((groups, vrows, tm), lambda i: (0, 0, i)),
                   pl.BlockSpec((groups, vrows, tm), lambda i: (0, 0, i)),
                   pl.BlockSpec((n_qt, groups, gate_rows, Q_BLOCK), lambda i: (i, 0, 0, 0)),
                   row(w["d"].shape[1])),
        compiler_params=_cparams(("parallel",)),
        name="in_proj",
    )(x, g, *weights, *tabs)


def _s5_prep_kernel(lr_ref, li_ref, ldt_ref, btr_ref, bti_ref, ctr_ref, cti_ref,
                    bre_ref, bim_ref, cre_ref, cim_ref, tab_ref):
    lr = lr_ref[...]
    li = li_ref[...]
    dt = jnp.exp(ldt_ref[...])
    mag = jnp.exp(lr * dt)
    ar = mag * jnp.cos(li * dt)
    ai = mag * jnp.sin(li * dt)
    nr = ar - 1.0
    ni = ai
    den = lr * lr + li * li
    f_re = (nr * lr + ni * li) / den
    f_im = (ni * lr - nr * li) / den
    btr = btr_ref[...]
    bti = bti_ref[...]
    rows = lax.broadcasted_iota(I32, btr.shape, 0) // S5_GROUP
    cols = lax.broadcasted_iota(I32, btr.shape, 1) // S5_STATE
    diag = rows == cols
    bre_ref[...] = jnp.where(diag, f_re * btr - f_im * bti, 0.0).astype(BF16)
    bim_ref[...] = jnp.where(diag, f_re * bti + f_im * btr, 0.0).astype(BF16)
    cre_ref[...] = jnp.where(diag, ctr_ref[...], 0.0).astype(BF16)
    cim_ref[...] = jnp.where(diag, cti_ref[...], 0.0).astype(BF16)
    pr = [ar]
    pi = [ai]
    for _ in range(SUBLANES - 1):
        pr_n = pr[-1] * ar - pi[-1] * ai
        pi_n = pr[-1] * ai + pi[-1] * ar
        pr.append(pr_n)
        pi.append(pi_n)
    rid = lax.broadcasted_iota(I32, (SUBLANES, lr.shape[1]), 0)
    zero = jnp.zeros((SUBLANES, lr.shape[1]), F32)
    p_re = zero
    p_im = zero
    for k in range(SUBLANES):
        p_re = jnp.where(rid == k, pr[k], p_re)
        p_im = jnp.where(rid == k, pi[k], p_im)
    for n, k in enumerate((1, 2, 4)):
        tab_ref[2 * n] = jnp.where(rid >= k, pr[k - 1], 0.0)
        tab_ref[2 * n + 1] = jnp.where(rid >= k, pi[k - 1], 0.0)
    tab_ref[6] = p_re
    tab_ref[7] = p_im


def _s5_prep(lam_re, lam_im, log_dt, b_re, b_im, c_re, c_im):
    g, p = lam_re.shape
    h = b_re.shape[-1]
    gp = g * p
    rowv = lambda a: a.reshape(1, gp)
    tile_t = lambda a: jnp.tile(a.reshape(h, gp), (g, 1))
    btr = tile_t(jnp.transpose(b_re, (2, 0, 1)))
    bti = tile_t(jnp.transpose(b_im, (2, 0, 1)))
    ctr = tile_t(jnp.transpose(c_re, (1, 0, 2)))
    cti = tile_t(jnp.transpose(c_im, (1, 0, 2)))
    mat = jax.ShapeDtypeStruct((g * h, gp), BF16)
    return pl.pallas_call(
        _s5_prep_kernel,
        out_shape=(mat, mat, mat, mat, jax.ShapeDtypeStruct((8, SUBLANES, gp), F32)),
        name="s5_prep",
    )(rowv(lam_re), rowv(lam_im), rowv(jnp.repeat(log_dt, p)), btr, bti, ctr, cti)


def _s5_kernel(u_ref, bre_ref, bim_ref, cre_ref, cim_ref, tab_ref, d_ref, wg_ref, bg_ref,
               y_ref, xr_ref, xi_ref, car_ref):
    @pl.when(pl.program_id(0) == 0)
    def _():
        car_ref[...] = jnp.zeros_like(car_ref)

    u = u_ref[...]
    ub = u.astype(BF16)
    xr_ref[...] = jnp.dot(ub, bre_ref[...], preferred_element_type=F32)
    xi_ref[...] = jnp.dot(ub, bim_ref[...], preferred_element_type=F32)
    tb = u.shape[0]

    def slab(r, carry):
        cr, ci = carry
        off = pl.multiple_of(r * SUBLANES, SUBLANES)
        xr = xr_ref[pl.ds(off, SUBLANES), :]
        xi = xi_ref[pl.ds(off, SUBLANES), :]
        for n, k in enumerate((1, 2, 4)):
            tr = tab_ref[2 * n]
            ti = tab_ref[2 * n + 1]
            sr = pltpu.roll(xr, k, 0)
            si = pltpu.roll(xi, k, 0)
            xr, xi = xr + tr * sr - ti * si, xi + tr * si + ti * sr
        pr = tab_ref[6]
        pi = tab_ref[7]
        xr, xi = xr + pr * cr - pi * ci, xi + pr * ci + pi * cr
        xr_ref[pl.ds(off, SUBLANES), :] = xr
        xi_ref[pl.ds(off, SUBLANES), :] = xi
        return xr[SUBLANES - 1:SUBLANES, :], xi[SUBLANES - 1:SUBLANES, :]

    cr, ci = lax.fori_loop(0, tb // SUBLANES, slab, (car_ref[0:1, :], car_ref[1:2, :]))
    car_ref[0:1, :] = cr
    car_ref[1:2, :] = ci
    y = _bdot_nt(xr_ref[...], cre_ref[...]) - _bdot_nt(xi_ref[...], cim_ref[...])
    y = y + d_ref[...] * u
    y = jax.nn.gelu(y)
    y = y * jax.nn.sigmoid(_bdot(y, wg_ref[...]) + bg_ref[...])
    y_ref[...] = y.astype(y_ref.dtype)


def _s5_mixer(u, prep, d_skip, w_glu, b_glu):
    seq, w = u.shape
    bre, bim, cre, cim, tab = prep
    gp = bre.shape[1]
    tb = min(seq, 512)
    full = lambda a: pl.BlockSpec(a.shape, lambda i: (0,) * a.ndim)
    return pl.pallas_call(
        _s5_kernel,
        out_shape=jax.ShapeDtypeStruct((seq, w), BF16),
        grid=(seq // tb,),
        in_specs=[pl.BlockSpec((tb, w), lambda i: (i, 0)), full(bre), full(bim), full(cre),
                  full(cim), full(tab), full(d_skip), full(w_glu), full(b_glu)],
        out_specs=pl.BlockSpec((tb, w), lambda i: (i, 0)),
        scratch_shapes=[pltpu.VMEM((tb, gp), F32), pltpu.VMEM((tb, gp), F32),
                        pltpu.VMEM((SUBLANES, gp), F32)],
        compiler_params=_cparams(("arbitrary",)),
        name="s5_mixer",
    )(u, bre, bim, cre, cim, tab, d_skip, w_glu, b_glu)


def _compress_kernel(x_ref, w1g_ref, w1_ref, w2k_ref, w2v_ref, pe_ref, kc_ref, vct_ref):
    seq = x_ref.shape[1]
    nc = seq // CMP_STRIDE
    groups = w1g_ref.shape[1]
    fdot = lambda a, b: jnp.dot(a, b, preferred_element_type=F32)

    def hidden(kind):
        taps = [x_ref[kind, pl.ds(l, nc, stride=CMP_STRIDE), :].astype(BF16) for l in range(CMP_STRIDE)]
        pew = _bdot(pe_ref[kind], w1_ref[kind])[0:1, :]
        out = []
        for g in range(groups):
            a = fdot(taps[0], w1g_ref[kind, g, 0])
            b = fdot(taps[0], w1g_ref[kind, g, CMP_STRIDE])
            for l in range(1, CMP_STRIDE):
                a = a + fdot(taps[l], w1g_ref[kind, g, l])
                b = b + fdot(taps[l], w1g_ref[kind, g, CMP_STRIDE + l])
            out.append(jax.nn.gelu(a + pltpu.roll(b, nc - 1, 0) + pew).astype(BF16))
        return out

    hk = hidden(0)
    kc = fdot(hk[0], w2k_ref[0])
    for g in range(1, groups):
        kc = kc + fdot(hk[g], w2k_ref[g])
    kc_ref[...] = kc.astype(kc_ref.dtype)

    hv = hidden(1)
    dh = w2v_ref.shape[0]
    rid = lax.broadcasted_iota(I32, (vct_ref.shape[1] - dh, nc), 0)
    tail = jnp.where(rid == 0, 1.0, 0.0).astype(vct_ref.dtype)
    for g in range(groups):
        vct_ref[g, 0:dh, :] = lax.dot_general(w2v_ref[...], hv[g], (((1,), (1,)), ((), ())),
                                              preferred_element_type=F32).astype(vct_ref.dtype)
        vct_ref[g, dh:, :] = tail


def _compress(x, w1g, w1, w2k, w2v, pe):
    seq = x.shape[1]
    nc = seq // CMP_STRIDE
    groups = w1g.shape[1]
    dh = w2v.shape[0]
    return pl.pallas_call(
        _compress_kernel,
        out_shape=(jax.ShapeDtypeStruct((nc, groups * dh), BF16),
                   jax.ShapeDtypeStruct((groups, dh + BF16_SUBLANES, nc), BF16)),
        compiler_params=pltpu.CompilerParams(vmem_limit_bytes=VMEM_LIMIT),
        name="compress",
    )(x, w1g, w1, w2k, w2v, pe)


def _nsa_pair_kernel(qt_ref, kc_ref, vct_ref, ks_ref, vst_ref, kw_ref, vwt_ref, g_ref,
                     out_ref, sc_ref, imp_ref, selb_ref, sa_ref, sb_ref, sc2_ref, sd_ref, *, nc, ns):
    qi = pl.program_id(0)
    groups = range(qt_ref.shape[0])
    dh, width = qt_ref.shape[2:]
    zeros = jnp.zeros((dh, width), qt_ref.dtype)
    qts = [jnp.concatenate([qt_ref[g, 0] if k == g else zeros for k in groups], axis=0) for g in groups]
    vrows = vst_ref.shape[1]
    lane = lax.broadcasted_iota(I32, (1, width), 1)
    t_row = qi * Q_BLOCK + (lane % Q_BLOCK)
    tq = qi * Q_BLOCK + lax.broadcasted_iota(I32, (1, Q_BLOCK), 1)
    all_heads = lambda a: jnp.concatenate([a] * NSA_HPG, axis=1)
    fdot = lambda a, b: jnp.dot(a, b, preferred_element_type=F32)

    pad = SUBLANES
    ch = min(CMP_CHUNK, nc)
    n_ch = (((qi + 1) * Q_BLOCK - CMP_BLOCK) // CMP_STRIDE) // ch + 1

    def cmp_scores(c, ms):
        off = pl.multiple_of(c * ch, ch)
        ci = off + lax.broadcasted_iota(I32, (ch, 1), 0)
        vis = ci * CMP_STRIDE + (CMP_BLOCK - 1) <= t_row
        out = []
        for g in groups:
            s = jnp.where(vis, fdot(kc_ref[pl.ds(off, ch), :], qts[g]), NEG_INF)
            sc_ref[g, pl.ds(off, ch), :] = s
            out.append(jnp.maximum(ms[g], jnp.max(s, axis=0, keepdims=True)))
        return tuple(out)

    ms = lax.fori_loop(0, n_ch, cmp_scores, tuple(jnp.full((1, width), NEG_INF, F32) for _ in groups))
    ms = [jnp.maximum(m, 0.1 * NEG_INF) for m in ms]

    def cmp_probs(c, accs):
        off = pl.multiple_of(c * ch, ch)
        out = []
        for g in groups:
            e = jnp.exp2(sc_ref[g, pl.ds(off, ch), :] - ms[g])
            sc_ref[g, pl.ds(off, ch), :] = e
            out.append(accs[g] + fdot(vct_ref[g, :, pl.ds(off, ch)], e.astype(BF16)))
        return tuple(out)

    accs = lax.fori_loop(0, n_ch, cmp_probs, tuple(jnp.zeros((vrows, width), F32) for _ in groups))
    inv_l = [1.0 / jnp.maximum(a[dh:dh + 1, :], 1e-30) for a in accs]
    o_cmp = [a[0:dh, :] * i for a, i in zip(accs, inv_l)]
    imp_ref[...] = jnp.zeros_like(imp_ref)

    def cmp_importance(c, _):
        off = pl.multiple_of(c * ch, ch)
        for g in groups:
            p = sc_ref[g, pl.ds(off, ch), :] * inv_l[g]
            imp = p[:, 0:Q_BLOCK]
            for h in range(1, NSA_HPG):
                imp = imp + p[:, h * Q_BLOCK:(h + 1) * Q_BLOCK]
            imp_ref[g, pl.ds(pl.multiple_of(pad + off, pad), ch), :] = imp
        return 0

    lax.fori_loop(0, n_ch, cmp_importance, 0)

    span_w = WINDOW + Q_BLOCK
    woff = pl.multiple_of(jnp.maximum(qi * Q_BLOCK - WINDOW, 0), Q_BLOCK)
    dpos = tq - (woff + lax.broadcasted_iota(I32, (span_w, 1), 0))
    ok = (dpos >= 0) & (dpos < WINDOW)
    win_bias = all_heads(jnp.where(ok, 0.0, NEG_INF))
    o_win = []
    for g in groups:
        sa_ref[g] = fdot(ks_ref[0:KV_TILE, :], qts[g])
        sb_ref[g] = fdot(ks_ref[KV_TILE:2 * KV_TILE, :], qts[g])
        s = fdot(kw_ref[pl.ds(woff, span_w), :], qts[g]) + win_bias
        p = jnp.exp2(s - jnp.max(s, axis=0, keepdims=True)).astype(BF16)
        acc = fdot(vwt_ref[g, :, pl.ds(woff, span_w)], p)
        o_win.append(acc[0:dh, :] / jnp.maximum(acc[dh:dh + 1, :], 1e-30))

    ratio = SLC_BLOCK // CMP_STRIDE
    span = ratio + CMP_BLOCK // CMP_STRIDE - 1
    lead = CMP_BLOCK // CMP_STRIDE - 1
    jb = lax.broadcasted_iota(I32, (ns, 1), 0)
    cur = tq // SLC_BLOCK
    bonus = ((jb == 0) | (jb == cur) | (jb == cur - 1)).astype(F32) * FORCE_BONUS
    jf = jb.astype(F32)
    for g in groups:
        imp_slc = imp_ref[g, pl.ds(pad - lead, ns, stride=ratio), :]
        for sft in range(1, span):
            imp_slc = imp_slc + imp_ref[g, pl.ds(pad - lead + sft, ns, stride=ratio), :]
        score = jnp.where(jb * SLC_BLOCK <= tq, imp_slc + bonus, NEG_INF)
        for _ in range(min(SLC_TOPK, ns)):
            mx = jnp.max(score, axis=0, keepdims=True)
            first = jnp.min(jnp.where(score == mx, jf, float(ns)), axis=0, keepdims=True)
            score = jnp.where(jf == first, -jnp.inf, score)
        selb_ref[g] = jnp.where(score == -jnp.inf, 0.0, NEG_INF)

    blocks = KV_TILE // SLC_BLOCK

    last_tile = ks_ref.shape[0] // KV_TILE - 1

    def tile_bias(g, kt):
        bias_t = selb_ref[g, pl.ds(pl.multiple_of(kt * blocks, blocks), blocks), :]
        bias = jnp.concatenate(
            [jnp.broadcast_to(bias_t[b:b + 1, :], (SLC_BLOCK, Q_BLOCK)) for b in range(blocks)], axis=0)
        return all_heads(bias)

    def scores(kt, dst_ref):
        kt = jnp.minimum(kt, last_tile)
        off = pl.multiple_of(kt * KV_TILE, KV_TILE)
        for g in groups:
            dst_ref[g] = fdot(ks_ref[pl.ds(off, KV_TILE), :], qts[g]) + tile_bias(g, kt)

    def consume(kt, src_ref, carry, masked=False):
        off = pl.multiple_of(kt * KV_TILE, KV_TILE)
        if masked:
            kpos = off + lax.broadcasted_iota(I32, (KV_TILE, 1), 0)
            causal = all_heads(jnp.where(kpos <= tq, 0.0, NEG_INF))
        out = []
        for g in groups:
            m, acc = carry[g]
            s = src_ref[g] + causal if masked else src_ref[g]
            m_new = jnp.maximum(m, jnp.max(s, axis=0, keepdims=True))
            p = jnp.exp2(s - m_new).astype(BF16)
            out.append((m_new, jnp.exp2(m - m_new) * acc + fdot(vst_ref[g, :, pl.ds(off, KV_TILE)], p)))
        return tuple(out)

    front, back = (sa_ref, sb_ref), (sc2_ref, sd_ref)
    for g in groups:
        sa_ref[g] = sa_ref[g] + tile_bias(g, 0)
        sb_ref[g] = sb_ref[g] + tile_bias(g, 1)

    def pair(k, cur, nxt, carry):
        scores(k + 2, nxt[0])
        carry = consume(k, cur[0], carry)
        scores(k + 3, nxt[1])
        return consume(k + 1, cur[1], carry)

    def last_pair(k, cur, carry):
        return consume(k + 1, cur[1], consume(k, cur[0], carry, True), True)

    init = tuple((jnp.full((1, width), NEG_INF, F32), jnp.zeros((vrows, width), F32)) for _ in groups)
    n_pairs = ((qi * Q_BLOCK) // KV_TILE) // 2
    carry = lax.fori_loop(0, n_pairs // 2,
                          lambda i, c: pair(4 * i + 2, back, front, pair(4 * i, front, back, c)), init)
    k0 = 4 * (n_pairs // 2)
    carry = lax.cond(n_pairs % 2 == 1,
                     lambda c: last_pair(k0 + 2, back, pair(k0, front, back, c)),
                     lambda c: last_pair(k0, front, c), carry)

    for g in groups:
        acc = carry[g][1]
        o_slc = acc[0:dh, :] / jnp.maximum(acc[dh:dh + 1, :], 1e-30)
        gate = jax.nn.sigmoid(g_ref[0, g])
        for h in range(NSA_HPG):
            hs = slice(h * Q_BLOCK, (h + 1) * Q_BLOCK)
            y = (gate[3 * h:3 * h + 1, :] * o_cmp[g][:, hs] + gate[3 * h + 1:3 * h + 2, :] * o_slc[:, hs]
                 + gate[3 * h + 2:3 * h + 3, :] * o_win[g][:, hs])
            row = (g * NSA_HPG + h) * dh
            out_ref[row:row + dh, :] = y.astype(out_ref.dtype)


def _nsa_pair(qt, kc, vct, ks, vst, kw, vwt, gates):
    g, nqt, dh, width = qt.shape
    seq = ks.shape[0]
    nc = kc.shape[0]
    ns = seq // SLC_BLOCK
    resident = lambda a: pl.BlockSpec(a.shape, lambda qi: (0,) * a.ndim, pipeline_mode=pl.Buffered(1))
    return pl.pallas_call(
        functools.partial(_nsa_pair_kernel, nc=nc, ns=ns),
        out_shape=jax.ShapeDtypeStruct((g * NSA_HPG * dh, seq), BF16),
        grid=(nqt,),
        in_specs=[pl.BlockSpec((g, 1, dh, width), lambda qi: (0, qi, 0, 0)),
                  resident(kc), resident(vct), resident(ks), resident(vst), resident(kw), resident(vwt),
                  pl.BlockSpec((1,) + gates.shape[1:], lambda qi: (qi, 0, 0, 0))],
        out_specs=pl.BlockSpec((g * NSA_HPG * dh, Q_BLOCK), lambda qi: (0, qi)),
        scratch_shapes=[pltpu.VMEM((g, nc, width), F32),
                        pltpu.VMEM((g, nc + 2 * SUBLANES, Q_BLOCK), F32),
                        pltpu.VMEM((g, ns, Q_BLOCK), F32),
                        ] + [pltpu.VMEM((g, KV_TILE, width), F32)] * 4,
        compiler_params=_cparams(("arbitrary",)),
        name="sparse_attention",
    )(qt, kc, vct, ks, vst, kw, vwt, gates)


def _split_bf16(x):
    hi = x.astype(BF16)
    return hi, (x - hi.astype(F32)).astype(BF16)


def _gla_kernel(h_ref, wa_ref, ba_ref, ng_ref, y_ref, st_ref):
    @pl.when(pl.program_id(0) == 0)
    def _():
        st_ref[...] = jnp.zeros_like(st_ref)

    kw = GLA_HEADS * GLA_DK
    vw = GLA_HEADS * GLA_DV
    c = GLA_CHUNK
    sb = min(GLA_SUPER, h_ref.shape[0])
    ri = lax.broadcasted_iota(I32, (sb, sb), 0)
    cj = lax.broadcasted_iota(I32, (sb, sb), 1)
    same_chunk = (ri // c) == (cj // c)
    tril = same_chunk & (ri >= cj)
    tril_m = jnp.where(tril, 1.0, 0.0).astype(BF16)
    chunk_m = jnp.where(same_chunk, 1.0, 0.0).astype(BF16)
    ones_c = jnp.ones((c, LANES), BF16)
    krow = lax.broadcasted_iota(I32, (kw, vw), 0) // GLA_DK
    vcol = lax.broadcasted_iota(I32, (kw, vw), 1) // GLA_DV
    blockdiag = krow == vcol
    klane = lax.broadcasted_iota(I32, (1, kw), 1) // GLA_DK
    vlane = lax.broadcasted_iota(I32, (1, vw), 1) // GLA_DV
    va = lax.broadcasted_iota(I32, (vw, vw), 0) // GLA_DV
    vb = lax.broadcasted_iota(I32, (vw, vw), 1) // GLA_DV
    head_avg = jnp.where(va == vb, 1.0 / GLA_DV, 0.0).astype(BF16)
    tn = (((0,), (0,)), ((), ()))

    state = st_ref[...]
    for blk in range(h_ref.shape[0] // sb):
        rows = slice(blk * sb, (blk + 1) * sb)
        q = h_ref[rows, 0:kw] * (GLA_DK ** -0.5)
        k = h_ref[rows, kw:2 * kw]
        v = h_ref[rows, 2 * kw:2 * kw + vw].astype(BF16)
        r = h_ref[rows, 2 * kw + vw:2 * kw + 2 * vw]
        a_low = h_ref[rows, 2 * kw + 2 * vw:2 * kw + 2 * vw + LANES]
        la = jax.nn.log_sigmoid(_hdot(a_low, wa_ref[...]) + ba_ref[...]) / GLA_TAU
        la_hi, la_lo = _split_bf16(la)
        sum_la = lambda mat: (jnp.dot(mat, la_hi, preferred_element_type=F32)
                              + jnp.dot(mat, la_lo, preferred_element_type=F32))
        bcum = sum_la(tril_m)
        b_last = sum_la(chunk_m)
        q_t = (q * jnp.exp(bcum)).astype(BF16)
        k_t = (k * jnp.exp(-bcum)).astype(BF16)
        k_d = (k * jnp.exp(b_last - bcum)).astype(BF16)
        o = jnp.zeros((sb, vw), F32)
        for hd in range(GLA_HEADS):
            attn = _bdot_nt(jnp.where(klane == hd, q_t, jnp.zeros_like(q_t)), k_t)
            attn = jnp.where(tril, attn, 0.0).astype(BF16)
            o = o + jnp.where(vlane == hd, jnp.dot(attn, v, preferred_element_type=F32), 0.0)
        inter = []
        for n in range(sb // c):
            cs = slice(n * c, (n + 1) * c)
            inter.append(jnp.dot(q_t[cs], state.astype(BF16), preferred_element_type=F32))
            kv = jnp.where(blockdiag, lax.dot_general(k_d[cs], v[cs], tn, preferred_element_type=F32), 0.0)
            tot = (lax.dot_general(la_hi[cs], ones_c, tn, preferred_element_type=F32)
                   + lax.dot_general(la_lo[cs], ones_c, tn, preferred_element_type=F32))
            decay = jnp.exp(tot)
            state = jnp.concatenate([decay] * (vw // LANES), axis=1) * state + kv
        o = o + jnp.concatenate(inter, axis=0)
        sq_hi, sq_lo = _split_bf16(o * o)
        ms = (jnp.dot(sq_hi, head_avg, preferred_element_type=F32)
              + jnp.dot(sq_lo, head_avg, preferred_element_type=F32))
        o = o * lax.rsqrt(ms + EPS)
        y_ref[rows, :] = (o * ng_ref[...] * (r * jax.nn.sigmoid(r))).astype(y_ref.dtype)
    st_ref[...] = state


def _gla_mixer(hd, w_a2p, b_a, norm_g):
    seq, wid = hd.shape
    tb = min(seq, 512)
    kw = GLA_HEADS * GLA_DK
    vw = GLA_HEADS * GLA_DV
    full = lambda a: pl.BlockSpec(a.shape, lambda i: (0, 0))
    return pl.pallas_call(
        _gla_kernel,
        out_shape=jax.ShapeDtypeStruct((seq, vw), BF16),
        grid=(seq // tb,),
        in_specs=[pl.BlockSpec((tb, wid), lambda i: (i, 0)), full(w_a2p), full(b_a), full(norm_g)],
        out_specs=pl.BlockSpec((tb, vw), lambda i: (i, 0)),
        scratch_shapes=[pltpu.VMEM((kw, vw), F32)],
        compiler_params=_cparams(("arbitrary",)),
        name="gla_mixer",
    )(hd, w_a2p, b_a, norm_g)


def _merge_kernel(x_ref, g_ref, ys_ref, yn_ref, yg_ref, wgm_ref, ws_ref, wn_ref, wg_ref, wo_ref, o_ref):
    x = x_ref[...]
    d = x.shape[1]
    xn = _rms(x, g_ref[...]).astype(BF16)
    gm = jax.nn.sigmoid(jnp.dot(xn, wgm_ref[...], preferred_element_type=F32))
    mixed = (gm[:, 0:d] * jnp.dot(ys_ref[...], ws_ref[...], preferred_element_type=F32)
             + gm[:, d:2 * d] * lax.dot_general(yn_ref[...], wn_ref[...], (((0,), (0,)), ((), ())),
                                                preferred_element_type=F32)
             + gm[:, 2 * d:3 * d] * jnp.dot(yg_ref[...], wg_ref[...], preferred_element_type=F32))
    o_ref[...] = x + _bdot(mixed, wo_ref[...])


def _merge(x, g, ys, yn, yg, wgm, ws, wn, wg, wo):
    seq, d = x.shape
    tm = min(seq, 512)
    row = lambda a: pl.BlockSpec((tm, a.shape[1]), lambda i: (i, 0))
    full = lambda a: pl.BlockSpec(a.shape, lambda i: (0, 0))
    return pl.pallas_call(
        _merge_kernel,
        out_shape=jax.ShapeDtypeStruct((seq, d), F32),
        grid=(seq // tm,),
        in_specs=[row(x), full(g), row(ys), pl.BlockSpec((yn.shape[0], tm), lambda i: (0, i)), row(yg),
                  full(wgm), full(ws), full(wn), full(wg), full(wo)],
        out_specs=row(x),
        compiler_params=_cparams(("parallel",)),
        name="merge",
    )(x, g, ys, yn, yg, wgm, ws, wn, wg, wo)


def _ffn_kernel(x_ref, xp_ref, g_ref, wug_ref, wuv_ref, cwg_ref, cwv_ref, cbg_ref, cbv_ref, wd_ref,
                fg_ref, o_ref, xn_ref, *, final):
    i = pl.program_id(0)
    j = pl.program_id(1)
    tm = x_ref.shape[0]
    halo = xp_ref.shape[0]

    @pl.when(j == 0)
    def _():
        xn_ref[0:halo, :] = jnp.where(i == 0, 0.0, _rms(xp_ref[...], g_ref[...])).astype(BF16)
        xn_ref[halo:halo + tm, :] = _rms(x_ref[...], g_ref[...]).astype(BF16)

    xn = xn_ref[...]

    def conv(w_ref, cw_ref, cb_ref):
        h = jnp.dot(xn, w_ref[...], preferred_element_type=F32)
        hc = cb_ref[...]
        for t in range(CONV_WIDTH):
            sh = CONV_WIDTH - 1 - t
            hs = h if sh == 0 else pltpu.roll(h, sh, 0)
            hc = hc + cw_ref[t:t + 1, :] * hs[halo:halo + tm, :]
        return hc

    act = jax.nn.gelu(conv(wug_ref, cwg_ref, cbg_ref)) * conv(wuv_ref, cwv_ref, cbv_ref)
    part = _bdot(act, wd_ref[...])

    @pl.when(j == 0)
    def _():
        o_ref[...] = x_ref[...] + part

    @pl.when(j > 0)
    def _():
        o_ref[...] = o_ref[...] + part

    if final:
        @pl.when(j == pl.num_programs(1) - 1)
        def _():
            o_ref[...] = _rms(o_ref[...], fg_ref[...])


def _ffn(x, g, w_up, conv_w, conv_b, w_down, final_g, final):
    seq, d = x.shape
    dff = w_down.shape[0]
    tm = min(seq, 1024)
    nj = 2
    tn = dff // nj
    halo = SUBLANES
    hb = tm // halo
    return pl.pallas_call(
        functools.partial(_ffn_kernel, final=final),
        out_shape=jax.ShapeDtypeStruct((seq, d), F32),
        grid=(seq // tm, nj),
        in_specs=[pl.BlockSpec((tm, d), lambda i, j: (i, 0)),
                  pl.BlockSpec((halo, d), lambda i, j: (jnp.maximum(i * hb - 1, 0), 0)),
                  pl.BlockSpec((1, d), lambda i, j: (0, 0)),
                  pl.BlockSpec((d, tn), lambda i, j: (0, j)),
                  pl.BlockSpec((d, tn), lambda i, j: (0, nj + j)),
                  pl.BlockSpec((CONV_WIDTH, tn), lambda i, j: (0, j)),
                  pl.BlockSpec((CONV_WIDTH, tn), lambda i, j: (0, nj + j)),
                  pl.BlockSpec((1, tn), lambda i, j: (0, j)),
                  pl.BlockSpec((1, tn), lambda i, j: (0, nj + j)),
                  pl.BlockSpec((tn, d), lambda i, j: (j, 0)),
                  pl.BlockSpec((1, d), lambda i, j: (0, 0))],
        out_specs=pl.BlockSpec((tm, d), lambda i, j: (i, 0)),
        scratch_shapes=[pltpu.VMEM((halo + tm, d), BF16)],
        compiler_params=_cparams(("parallel", "arbitrary")),
        name="conv_ffn",
    )(x, x, g, w_up, w_up, conv_w, conv_w, conv_b, conv_b, w_down, final_g)


def _split_w_in(w):
    d = w.shape[0]
    s5w = d // 4
    nsa_w = NSA_HEADS * HEAD_DIM
    kvw = NSA_KV_GROUPS * HEAD_DIM
    kw = GLA_HEADS * GLA_DK
    vw = GLA_HEADS * GLA_DV
    sizes = (s5w, nsa_w, kvw, kvw, kvw, kvw, kvw, kvw, 3 * NSA_HEADS, kw, kw, vw, 16, vw, 3 * d)
    assert sum(sizes) == w.shape[1]
    offs = [0]
    for sz in sizes:
        offs.append(offs[-1] + sz)
    col = lambda n: w[:, offs[n]:offs[n + 1]]
    (u, q, kc, vc, ks, vs, kwn, vwn, gn, gq, gk, gv, ga, gr, gm) = [col(n) for n in range(len(sizes))]
    padl = lambda a: jnp.pad(a, ((0, 0), (0, LANES - a.shape[1])))
    per_group = 3 * NSA_HPG
    gates_t = jnp.pad(gn.T.reshape(NSA_KV_GROUPS, per_group, d),
                      ((0, 0), (0, BF16_SUBLANES - per_group), (0, 0))).reshape(-1, d)
    out = dict(u=u, k=jnp.concatenate([kc, ks, kwn], axis=1), vc=vc,
               d=jnp.concatenate([gq, gk, gv, gr, padl(ga)], axis=1),
               qt=q.T, vt=jnp.concatenate([vs, vwn], axis=1).T, gt=gates_t, gm=gm)
    return {name: a.astype(BF16) for name, a in out.items()}


def kernel(x, positions, norm1_g, w_in, s5_lam_re, s5_lam_im, s5_log_dt, s5_b_re, s5_b_im, s5_c_re, s5_c_im, s5_d, s5_w_glu, s5_b_glu, nsa_pe_k, nsa_pe_v, nsa_ck_w1, nsa_ck_w2, nsa_cv_w1, nsa_cv_w2, gla_w_a2, gla_b_a, gla_norm_g, w_br_s5, w_br_nsa, w_br_gla, w_out, norm2_g, ffn_w_up, ffn_conv_w, ffn_conv_b, ffn_w_down, final_g):
    bsz, seq, d = x.shape
    depth = w_in.shape[0]
    assert bsz == 1 and seq % KV_TILE == 0 and seq >= 2 * KV_TILE and d % LANES == 0
    g = NSA_KV_GROUPS
    dh = HEAD_DIM
    row = lambda a: a.reshape(1, -1)

    xs = x.reshape(seq, d)
    tabs = _rope_tables(positions.reshape(seq))

    for l in range(depth):
        w = _split_w_in(w_in[l])
        wgm = w["gm"]
        g1 = row(norm1_g[l])
        u_s5, qt, x_cmp, ks, kwn, vst, vwt, gates, hd = _project(xs, g1, w, tabs)

        prep = _s5_prep(s5_lam_re[l], s5_lam_im[l], s5_log_dt[l], s5_b_re[l], s5_b_im[l],
                        s5_c_re[l], s5_c_im[l])
        y_s5 = _s5_mixer(u_s5, prep, row(s5_d[l]), s5_w_glu[l].astype(BF16), row(s5_b_glu[l]))

        pe = jnp.stack([nsa_pe_k[l].reshape(1, -1), nsa_pe_v[l].reshape(1, -1)])
        pe = jnp.broadcast_to(pe, (2, SUBLANES, pe.shape[-1]))
        w1 = jnp.stack([nsa_ck_w1[l], nsa_cv_w1[l]]).astype(BF16)
        taps = w1.reshape(2, 1, CMP_BLOCK, dh, w1.shape[-1])
        w1g = jnp.concatenate(
            [jnp.pad(taps, ((0, 0), (0, 0), (0, 0), (gi * dh, (g - 1 - gi) * dh), (0, 0))) for gi in range(g)],
            axis=1)
        w2k = jnp.stack([jnp.pad(nsa_ck_w2[l], ((0, 0), (gi * dh, (g - 1 - gi) * dh))) for gi in range(g)])
        kc, vct = _compress(x_cmp, w1g, w1, w2k.astype(BF16), nsa_cv_w2[l].T.astype(BF16), pe)
        y_nsa = _nsa_pair(qt, kc, vct, ks, vst, kwn, vwt, gates)

        w_a2p = jnp.pad(gla_w_a2[l], ((0, LANES - gla_w_a2.shape[1]), (0, 0)))
        y_gla = _gla_mixer(hd, w_a2p, row(gla_b_a[l]), row(gla_norm_g[l]))

        xs = _merge(xs, g1, y_s5, y_nsa, y_gla, wgm, w_br_s5[l].astype(BF16), w_br_nsa[l].astype(BF16),
                    w_br_gla[l].astype(BF16), w_out[l].astype(BF16))
        xs = _ffn(xs, row(norm2_g[l]), ffn_w_up[l].astype(BF16), ffn_conv_w[l], row(ffn_conv_b[l]),
                  ffn_w_down[l].astype(BF16), row(final_g), final=(l == depth - 1))
    return xs.reshape(bsz, seq, d)
```

```python
import functools
import math

import jax
import jax.numpy as jnp
from jax import lax
from jax.experimental import pallas as pl
from jax.experimental.pallas import tpu as pltpu

F32 = jnp.float32
BF16 = jnp.bfloat16
I32 = jnp.int32

EPS = 1e-6
NEG_INF = -1e30

S5_GROUP = 16
S5_STATE = 64
NSA_HEADS = 8
NSA_KV_GROUPS = 2
NSA_HPG = NSA_HEADS // NSA_KV_GROUPS
HEAD_DIM = 64
ROPE_DIM = 16
ROPE_THETA = 500000.0
CMP_BLOCK = 32
CMP_STRIDE = 16
SLC_BLOCK = 64
SLC_TOPK = 16
WINDOW = 512
Q_BLOCK = 128
FORCE_BONUS = 1.0e4
GLA_HEADS = 4
GLA_DK = 32
GLA_DV = 64
GLA_TAU = 16.0
GLA_CHUNK = 64
GLA_SUPER = 256
CONV_WIDTH = 3

LANES = 128
SUBLANES = 8
KV_TILE = 512
CMP_CHUNK = 256
BF16_SUBLANES = 16
VMEM_LIMIT = 56 * 1024 * 1024


def _cparams(sem):
    return pltpu.CompilerParams(dimension_semantics=sem, vmem_limit_bytes=VMEM_LIMIT)


def _of_layer(a, l):
    return pl.BlockSpec((None,) + a.shape[1:], lambda *_: (l,) + (0,) * (a.ndim - 1))


def _bdot(a, b):
    return jnp.dot(a.astype(BF16), b.astype(BF16), preferred_element_type=F32)


def _bdot_nt(a, b):
    return lax.dot_general(a.astype(BF16), b.astype(BF16), (((1,), (1,)), ((), ())),
                           preferred_element_type=F32)


def _hdot(a, b):
    return jnp.dot(a, b, preferred_element_type=F32, precision=lax.Precision.HIGHEST)


def _rms(x, g):
    return x * lax.rsqrt(jnp.mean(x * x, axis=-1, keepdims=True) + EPS) * g


def _rope_tab_kernel(pos_ref, post_ref, invf_ref, invft_ref, c_ref, s1_ref, s2_ref, ct_ref, s1t_ref, s2t_ref):
    half = ROPE_DIM // 2

    def tables(ang, dim, c_out, s1_out, s2_out):
        cosv = jnp.cos(ang)
        sinv = jnp.sin(ang)
        c_out[...] = jnp.where(dim < ROPE_DIM, cosv, 1.0)
        s1_out[...] = jnp.where(dim < half, 0.0, jnp.where(dim < ROPE_DIM, sinv, 0.0))
        s2_out[...] = jnp.where(dim < half, -sinv, 0.0)

    ang = pos_ref[...].astype(F32) * invf_ref[...]
    tables(ang, lax.broadcasted_iota(I32, ang.shape, 1) % HEAD_DIM, c_ref, s1_ref, s2_ref)
    ang_t = invft_ref[...] * post_ref[...].astype(F32)
    tables(ang_t, lax.broadcasted_iota(I32, ang_t.shape, 0), ct_ref, s1t_ref, s2t_ref)


def _rope_tables(positions):
    seq = positions.shape[0]
    tm = min(seq, 2048)
    inv_freq = ROPE_THETA ** (-jnp.arange(0, ROPE_DIM, 2, dtype=F32) / ROPE_DIM)
    dim = jnp.arange(LANES) % HEAD_DIM
    invf = jnp.where(dim < ROPE_DIM, inv_freq[dim % (ROPE_DIM // 2)], 0.0)
    tab = jax.ShapeDtypeStruct((seq, LANES), F32)
    tab_t = jax.ShapeDtypeStruct((HEAD_DIM, seq), F32)
    return pl.pallas_call(
        _rope_tab_kernel,
        out_shape=(tab, tab, tab, tab_t, tab_t, tab_t),
        grid=(seq // tm,),
        in_specs=[pl.BlockSpec((tm, 1), lambda i: (i, 0)),
                  pl.BlockSpec((1, tm), lambda i: (0, i)),
                  pl.BlockSpec((1, LANES), lambda i: (0, 0)),
                  pl.BlockSpec((HEAD_DIM, 1), lambda i: (0, 0))],
        out_specs=tuple([pl.BlockSpec((tm, LANES), lambda i: (i, 0))] * 3
                        + [pl.BlockSpec((HEAD_DIM, tm), lambda i: (0, i))] * 3),
        compiler_params=_cparams(("parallel",)),
        name="rope_tables",
    )(positions.reshape(seq, 1), positions.reshape(1, seq), invf.reshape(1, LANES),
      invf[:HEAD_DIM].reshape(HEAD_DIM, 1))


def _proj_kernel(x_ref, g_ref, wu_ref, wk_ref, wvc_ref, wd_ref, wqt_ref, wvt_ref, wgt_ref,
                 c_ref, s1_ref, s2_ref, ct_ref, s1t_ref, s2t_ref,
                 u_ref, qt_ref, cmp_ref, ks_ref, kw_ref, vst_ref, vwt_ref, gt_ref, hd_ref):
    xn = _rms(x_ref[...], g_ref[...]).astype(BF16)
    tm = xn.shape[0]
    n_qt = tm // Q_BLOCK
    half = ROPE_DIM // 2
    fdot = lambda a, b: jnp.dot(a, b, preferred_element_type=F32)
    tdot = lambda w, a: lax.dot_general(w, a, (((1,), (1,)), ((), ())), preferred_element_type=F32)

    u_ref[...] = fdot(xn, wu_ref[...])
    hd_ref[...] = fdot(xn, wd_ref[...])
    cmp_ref[1] = fdot(xn, wvc_ref[...])
    hk = fdot(xn, wk_ref[...])
    c, s1, s2 = c_ref[...], s1_ref[...], s2_ref[...]
    for j, out in enumerate((cmp_ref.at[0], ks_ref, kw_ref)):
        piece = hk[:, j * LANES:(j + 1) * LANES]
        rot = piece * c + pltpu.roll(piece, half, 1) * s1 + pltpu.roll(piece, LANES - half, 1) * s2
        out[...] = rot.astype(out.dtype)

    q_t = tdot(wqt_ref[...], xn)
    ct, s1t, s2t = ct_ref[...], s1t_ref[...], s2t_ref[...]
    dh = ct.shape[0]
    for gh in range(q_t.shape[0] // dh):
        blk = q_t[gh * dh:(gh + 1) * dh, :]
        rot = blk * ct + pltpu.roll(blk, half, 0) * s1t + pltpu.roll(blk, dh - half, 0) * s2t
        rot = (rot * (dh ** -0.5 * math.log2(math.e))).astype(qt_ref.dtype)
        g, h = divmod(gh, NSA_HPG)
        for j in range(n_qt):
            qt_ref[g, j, :, h * Q_BLOCK:(h + 1) * Q_BLOCK] = rot[:, j * Q_BLOCK:(j + 1) * Q_BLOCK]

    v_t = tdot(wvt_ref[...], xn)
    rid = lax.broadcasted_iota(I32, (vst_ref.shape[1] - dh, tm), 0)
    tail = jnp.where(rid == 0, 1.0, 0.0).astype(vst_ref.dtype)
    for b, out in enumerate((vst_ref, vwt_ref)):
        for g in range(out.shape[0]):
            row = (b * out.shape[0] + g) * dh
            out[g, 0:dh, :] = v_t[row:row + dh, :].astype(out.dtype)
            out[g, dh:, :] = tail

    g_t = tdot(wgt_ref[...], xn)
    rows = gt_ref.shape[2]
    for j in range(n_qt):
        for g in range(gt_ref.shape[1]):
            gt_ref[j, g] = g_t[g * rows:(g + 1) * rows, j * Q_BLOCK:(j + 1) * Q_BLOCK]


def _project(x, g, w, tabs, l):
    seq, d = x.shape
    tm = min(seq, 512)
    n_qt = tm // Q_BLOCK
    groups = NSA_KV_GROUPS
    vrows = HEAD_DIM + BF16_SUBLANES
    gate_rows = w["gt"].shape[1] // groups
    row = lambda wid: pl.BlockSpec((tm, wid), lambda i: (i, 0))
    col = lambda r: pl.BlockSpec((r, tm), lambda i: (0, i))
    weights = [w[k] for k in ("u", "k", "vc", "d", "qt", "vt", "gt")]
    sds = jax.ShapeDtypeStruct
    return pl.pallas_call(
        _proj_kernel,
        out_shape=(sds((seq, w["u"].shape[2]), F32),
                   sds((groups, seq // Q_BLOCK, HEAD_DIM, NSA_HPG * Q_BLOCK), BF16),
                   sds((2, seq, LANES), F32),
                   sds((seq, LANES), BF16), sds((seq, LANES), BF16),
                   sds((groups, vrows, seq), BF16), sds((groups, vrows, seq), BF16),
                   sds((seq // Q_BLOCK, groups, gate_rows, Q_BLOCK), F32),
                   sds((seq, w["d"].shape[2]), F32)),
        grid=(seq // tm,),
        in_specs=[row(d), _of_layer(g, l)] + [_of_layer(a, l) for a in weights]
                 + [row(LANES)] * 3 + [col(HEAD_DIM)] * 3,
        out_specs=(row(w["u"].shape[2]),
                   pl.BlockSpec((groups, n_qt, HEAD_DIM, NSA_HPG * Q_BLOCK), lambda i: (0, i, 0, 0)),
                   pl.BlockSpec((2, tm, LANES), lambda i: (0, i, 0)), row(LANES), row(LANES),
                   pl.BlockSpec((groups, vrows, tm), lambda i: (0, 0, i)),
                   pl.BlockSpec((groups, vrows, tm), lambda i: (0, 0, i)),
                   pl.BlockSpec((n_qt, groups, gate_rows, Q_BLOCK), lambda i: (i, 0, 0, 0)),
                   row(w["d"].shape[2])),
        compiler_params=_cparams(("parallel",)),
        name="in_proj",
    )(x, g, *weights, *tabs)


def _s5_prep_kernel(lr_ref, li_ref, ldt_ref, btr_ref, bti_ref, ctr_ref, cti_ref,
                    bre_ref, bim_ref, cre_ref, cim_ref, tab_ref):
    lr = lr_ref[...]
    li = li_ref[...]
    dt = jnp.exp(ldt_ref[...])
    mag = jnp.exp(lr * dt)
    ar = mag * jnp.cos(li * dt)
    ai = mag * jnp.sin(li * dt)
    nr = ar - 1.0
    ni = ai
    den = lr * lr + li * li
    f_re = (nr * lr + ni * li) / den
    f_im = (ni * lr - nr * li) / den
    btr = btr_ref[...]
    bti = bti_ref[...]
    rows = lax.broadcasted_iota(I32, btr.shape, 0) // S5_GROUP
    cols = lax.broadcasted_iota(I32, btr.shape, 1) // S5_STATE
    diag = rows == cols
    bre_ref[...] = jnp.where(diag, f_re * btr - f_im * bti, 0.0).astype(BF16)
    bim_ref[...] = jnp.where(diag, f_re * bti + f_im * btr, 0.0).astype(BF16)
    cre_ref[...] = jnp.where(diag, ctr_ref[...], 0.0).astype(BF16)
    cim_ref[...] = jnp.where(diag, cti_ref[...], 0.0).astype(BF16)
    pr = [ar]
    pi = [ai]
    for _ in range(SUBLANES - 1):
        pr_n = pr[-1] * ar - pi[-1] * ai
        pi_n = pr[-1] * ai + pi[-1] * ar
        pr.append(pr_n)
        pi.append(pi_n)
    rid = lax.broadcasted_iota(I32, (SUBLANES, lr.shape[1]), 0)
    zero = jnp.zeros((SUBLANES, lr.shape[1]), F32)
    p_re = zero
    p_im = zero
    for k in range(SUBLANES):
        p_re = jnp.where(rid == k, pr[k], p_re)
        p_im = jnp.where(rid == k, pi[k], p_im)
    for n, k in enumerate((1, 2, 4)):
        tab_ref[2 * n] = jnp.where(rid >= k, pr[k - 1], 0.0)
        tab_ref[2 * n + 1] = jnp.where(rid >= k, pi[k - 1], 0.0)
    tab_ref[6] = p_re
    tab_ref[7] = p_im


def _s5_prep(lam_re, lam_im, log_dt, b_re, b_im, c_re, c_im):
    g, p = lam_re.shape
    h = b_re.shape[-1]
    gp = g * p
    rowv = lambda a: a.reshape(1, gp)
    tile_t = lambda a: jnp.tile(a.reshape(h, gp), (g, 1))
    btr = tile_t(jnp.transpose(b_re, (2, 0, 1)))
    bti = tile_t(jnp.transpose(b_im, (2, 0, 1)))
    ctr = tile_t(jnp.transpose(c_re, (1, 0, 2)))
    cti = tile_t(jnp.transpose(c_im, (1, 0, 2)))
    mat = jax.ShapeDtypeStruct((g * h, gp), BF16)
    return pl.pallas_call(
        _s5_prep_kernel,
        out_shape=(mat, mat, mat, mat, jax.ShapeDtypeStruct((8, SUBLANES, gp), F32)),
        name="s5_prep",
    )(rowv(lam_re), rowv(lam_im), rowv(jnp.repeat(log_dt, p)), btr, bti, ctr, cti)


def _s5_kernel(u_ref, bre_ref, bim_ref, cre_ref, cim_ref, tab_ref, d_ref, wg_ref, bg_ref,
               y_ref, xr_ref, xi_ref, car_ref):
    @pl.when(pl.program_id(0) == 0)
    def _():
        car_ref[...] = jnp.zeros_like(car_ref)

    u = u_ref[...]
    ub = u.astype(BF16)
    xr_ref[...] = jnp.dot(ub, bre_ref[...], preferred_element_type=F32)
    xi_ref[...] = jnp.dot(ub, bim_ref[...], preferred_element_type=F32)
    tb = u.shape[0]

    def slab(r, carry):
        cr, ci = carry
        off = pl.multiple_of(r * SUBLANES, SUBLANES)
        xr = xr_ref[pl.ds(off, SUBLANES), :]
        xi = xi_ref[pl.ds(off, SUBLANES), :]
        for n, k in enumerate((1, 2, 4)):
            tr = tab_ref[2 * n]
            ti = tab_ref[2 * n + 1]
            sr = pltpu.roll(xr, k, 0)
            si = pltpu.roll(xi, k, 0)
            xr, xi = xr + tr * sr - ti * si, xi + tr * si + ti * sr
        pr = tab_ref[6]
        pi = tab_ref[7]
        xr, xi = xr + pr * cr - pi * ci, xi + pr * ci + pi * cr
        xr_ref[pl.ds(off, SUBLANES), :] = xr
        xi_ref[pl.ds(off, SUBLANES), :] = xi
        return xr[SUBLANES - 1:SUBLANES, :], xi[SUBLANES - 1:SUBLANES, :]

    cr, ci = lax.fori_loop(0, tb // SUBLANES, slab, (car_ref[0:1, :], car_ref[1:2, :]))
    car_ref[0:1, :] = cr
    car_ref[1:2, :] = ci
    y = _bdot_nt(xr_ref[...], cre_ref[...]) - _bdot_nt(xi_ref[...], cim_ref[...])
    y = y + d_ref[...] * u
    y = jax.nn.gelu(y)
    y = y * jax.nn.sigmoid(_bdot(y, wg_ref[...]) + bg_ref[...])
    y_ref[...] = y.astype(y_ref.dtype)


def _s5_mixer(u, prep, d_skip, w_glu, b_glu, l):
    seq, w = u.shape
    bre, bim, cre, cim, tab = prep
    gp = bre.shape[1]
    tb = min(seq, 512)
    full = lambda a: pl.BlockSpec(a.shape, lambda i: (0,) * a.ndim)
    return pl.pallas_call(
        _s5_kernel,
        out_shape=jax.ShapeDtypeStruct((seq, w), BF16),
        grid=(seq // tb,),
        in_specs=[pl.BlockSpec((tb, w), lambda i: (i, 0)), full(bre), full(bim), full(cre),
                  full(cim), full(tab), _of_layer(d_skip, l), _of_layer(w_glu, l), _of_layer(b_glu, l)],
        out_specs=pl.BlockSpec((tb, w), lambda i: (i, 0)),
        scratch_shapes=[pltpu.VMEM((tb, gp), F32), pltpu.VMEM((tb, gp), F32),
                        pltpu.VMEM((SUBLANES, gp), F32)],
        compiler_params=_cparams(("arbitrary",)),
        name="s5_mixer",
    )(u, bre, bim, cre, cim, tab, d_skip, w_glu, b_glu)


def _compress_kernel(x_ref, w1g_ref, w1_ref, w2k_ref, w2v_ref, pe_ref, kc_ref, vct_ref):
    seq = x_ref.shape[1]
    nc = seq // CMP_STRIDE
    groups = w1g_ref.shape[1]
    fdot = lambda a, b: jnp.dot(a, b, preferred_element_type=F32)

    def hidden(kind):
        taps = [x_ref[kind, pl.ds(l, nc, stride=CMP_STRIDE), :].astype(BF16) for l in range(CMP_STRIDE)]
        pew = _bdot(pe_ref[kind], w1_ref[kind])[0:1, :]
        out = []
        for g in range(groups):
            a = fdot(taps[0], w1g_ref[kind, g, 0])
            b = fdot(taps[0], w1g_ref[kind, g, CMP_STRIDE])
            for l in range(1, CMP_STRIDE):
                a = a + fdot(taps[l], w1g_ref[kind, g, l])
                b = b + fdot(taps[l], w1g_ref[kind, g, CMP_STRIDE + l])
            out.append(jax.nn.gelu(a + pltpu.roll(b, nc - 1, 0) + pew).astype(BF16))
        return out

    hk = hidden(0)
    kc = fdot(hk[0], w2k_ref[0])
    for g in range(1, groups):
        kc = kc + fdot(hk[g], w2k_ref[g])
    kc_ref[...] = kc.astype(kc_ref.dtype)

    hv = hidden(1)
    dh = w2v_ref.shape[0]
    rid = lax.broadcasted_iota(I32, (vct_ref.shape[1] - dh, nc), 0)
    tail = jnp.where(rid == 0, 1.0, 0.0).astype(vct_ref.dtype)
    for g in range(groups):
        vct_ref[g, 0:dh, :] = lax.dot_general(w2v_ref[...], hv[g], (((1,), (1,)), ((), ())),
                                              preferred_element_type=F32).astype(vct_ref.dtype)
        vct_ref[g, dh:, :] = tail


def _compress(x, w1g, w1, w2k, w2v, pe, l):
    seq = x.shape[1]
    nc = seq // CMP_STRIDE
    groups = w1g.shape[2]
    dh = w2v.shape[1]
    out_shape = (jax.ShapeDtypeStruct((nc, groups * dh), BF16),
                 jax.ShapeDtypeStruct((groups, dh + BF16_SUBLANES, nc), BF16))
    whole = lambda a, **kw: pl.BlockSpec(a.shape, lambda i: (0,) * len(a.shape), **kw)
    once = dict(pipeline_mode=pl.Buffered(1))
    layer = lambda a: pl.BlockSpec((None,) + a.shape[1:], lambda i: (l,) + (0,) * (a.ndim - 1), **once)
    return pl.pallas_call(
        _compress_kernel,
        out_shape=out_shape,
        grid=(1,),
        in_specs=[whole(x, **once)] + [layer(a) for a in (w1g, w1, w2k, w2v, pe)],
        out_specs=tuple(whole(o) for o in out_shape),
        compiler_params=_cparams(("arbitrary",)),
        name="compress",
    )(x, w1g, w1, w2k, w2v, pe)


def _nsa_pair_kernel(qt_ref, kc_ref, vct_ref, ks_ref, vst_ref, kw_ref, vwt_ref, g_ref,
                     out_ref, sc_ref, imp_ref, selb_ref, sa_ref, sb_ref, sc2_ref, sd_ref, *, nc, ns):
    qi = pl.program_id(0)
    groups = range(qt_ref.shape[0])
    dh, width = qt_ref.shape[2:]
    zeros = jnp.zeros((dh, width), qt_ref.dtype)
    qts = [jnp.concatenate([qt_ref[g, 0] if k == g else zeros for k in groups], axis=0) for g in groups]
    vrows = vst_ref.shape[1]
    lane = lax.broadcasted_iota(I32, (1, width), 1)
    t_row = qi * Q_BLOCK + (lane % Q_BLOCK)
    tq = qi * Q_BLOCK + lax.broadcasted_iota(I32, (1, Q_BLOCK), 1)
    all_heads = lambda a: jnp.concatenate([a] * NSA_HPG, axis=1)
    fdot = lambda a, b: jnp.dot(a, b, preferred_element_type=F32)

    pad = SUBLANES
    ch = min(CMP_CHUNK, nc)
    n_ch = (((qi + 1) * Q_BLOCK - CMP_BLOCK) // CMP_STRIDE) // ch + 1

    def cmp_scores(c, ms):
        off = pl.multiple_of(c * ch, ch)
        ci = off + lax.broadcasted_iota(I32, (ch, 1), 0)
        vis = ci * CMP_STRIDE + (CMP_BLOCK - 1) <= t_row
        out = []
        for g in groups:
            s = jnp.where(vis, fdot(kc_ref[pl.ds(off, ch), :], qts[g]), NEG_INF)
            sc_ref[g, pl.ds(off, ch), :] = s
            out.append(jnp.maximum(ms[g], jnp.max(s, axis=0, keepdims=True)))
        return tuple(out)

    ms = lax.fori_loop(0, n_ch, cmp_scores, tuple(jnp.full((1, width), NEG_INF, F32) for _ in groups))
    ms = [jnp.maximum(m, 0.1 * NEG_INF) for m in ms]

    def cmp_probs(c, accs):
        off = pl.multiple_of(c * ch, ch)
        out = []
        for g in groups:
            e = jnp.exp2(sc_ref[g, pl.ds(off, ch), :] - ms[g])
            sc_ref[g, pl.ds(off, ch), :] = e
            out.append(accs[g] + fdot(vct_ref[g, :, pl.ds(off, ch)], e.astype(BF16)))
        return tuple(out)

    accs = lax.fori_loop(0, n_ch, cmp_probs, tuple(jnp.zeros((vrows, width), F32) for _ in groups))
    inv_l = [1.0 / jnp.maximum(a[dh:dh + 1, :], 1e-30) for a in accs]
    o_cmp = [a[0:dh, :] * i for a, i in zip(accs, inv_l)]
    imp_ref[...] = jnp.zeros_like(imp_ref)

    def cmp_importance(c, _):
        off = pl.multiple_of(c * ch, ch)
        for g in groups:
            p = sc_ref[g, pl.ds(off, ch), :] * inv_l[g]
            imp = p[:, 0:Q_BLOCK]
            for h in range(1, NSA_HPG):
                imp = imp + p[:, h * Q_BLOCK:(h + 1) * Q_BLOCK]
            imp_ref[g, pl.ds(pl.multiple_of(pad + off, pad), ch), :] = imp
        return 0

    lax.fori_loop(0, n_ch, cmp_importance, 0)

    span_w = WINDOW + Q_BLOCK
    woff = pl.multiple_of(jnp.maximum(qi * Q_BLOCK - WINDOW, 0), Q_BLOCK)
    dpos = tq - (woff + lax.broadcasted_iota(I32, (span_w, 1), 0))
    ok = (dpos >= 0) & (dpos < WINDOW)
    win_bias = all_heads(jnp.where(ok, 0.0, NEG_INF))
    o_win = []
    for g in groups:
        sa_ref[g] = fdot(ks_ref[0:KV_TILE, :], qts[g])
        sb_ref[g] = fdot(ks_ref[KV_TILE:2 * KV_TILE, :], qts[g])
        s = fdot(kw_ref[pl.ds(woff, span_w), :], qts[g]) + win_bias
        p = jnp.exp2(s - jnp.max(s, axis=0, keepdims=True)).astype(BF16)
        acc = fdot(vwt_ref[g, :, pl.ds(woff, span_w)], p)
        o_win.append(acc[0:dh, :] / jnp.maximum(acc[dh:dh + 1, :], 1e-30))

    ratio = SLC_BLOCK // CMP_STRIDE
    span = ratio + CMP_BLOCK // CMP_STRIDE - 1
    lead = CMP_BLOCK // CMP_STRIDE - 1
    jb = lax.broadcasted_iota(I32, (ns, 1), 0)
    cur = tq // SLC_BLOCK
    bonus = ((jb == 0) | (jb == cur) | (jb == cur - 1)).astype(F32) * FORCE_BONUS
    jf = jb.astype(F32)
    for g in groups:
        imp_slc = imp_ref[g, pl.ds(pad - lead, ns, stride=ratio), :]
        for sft in range(1, span):
            imp_slc = imp_slc + imp_ref[g, pl.ds(pad - lead + sft, ns, stride=ratio), :]
        score = jnp.where(jb * SLC_BLOCK <= tq, imp_slc + bonus, NEG_INF)
        for _ in range(min(SLC_TOPK, ns)):
            mx = jnp.max(score, axis=0, keepdims=True)
            first = jnp.min(jnp.where(score == mx, jf, float(ns)), axis=0, keepdims=True)
            score = jnp.where(jf == first, -jnp.inf, score)
        selb_ref[g] = jnp.where(score == -jnp.inf, 0.0, NEG_INF)

    blocks = KV_TILE // SLC_BLOCK

    last_tile = ks_ref.shape[0] // KV_TILE - 1

    def tile_bias(g, kt):
        bias_t = selb_ref[g, pl.ds(pl.multiple_of(kt * blocks, blocks), blocks), :]
        bias = jnp.concatenate(
            [jnp.broadcast_to(bias_t[b:b + 1, :], (SLC_BLOCK, Q_BLOCK)) for b in range(blocks)], axis=0)
        return all_heads(bias)

    def scores(kt, dst_ref):
        kt = jnp.minimum(kt, last_tile)
        off = pl.multiple_of(kt * KV_TILE, KV_TILE)
        for g in groups:
            dst_ref[g] = fdot(ks_ref[pl.ds(off, KV_TILE), :], qts[g]) + tile_bias(g, kt)

    def consume(kt, src_ref, carry, masked=False):
        off = pl.multiple_of(kt * KV_TILE, KV_TILE)
        if masked:
            kpos = off + lax.broadcasted_iota(I32, (KV_TILE, 1), 0)
            causal = all_heads(jnp.where(kpos <= tq, 0.0, NEG_INF))
        out = []
        for g in groups:
            m, acc = carry[g]
            s = src_ref[g] + causal if masked else src_ref[g]
            m_new = jnp.maximum(m, jnp.max(s, axis=0, keepdims=True))
            p = jnp.exp2(s - m_new).astype(BF16)
            out.append((m_new, jnp.exp2(m - m_new) * acc + fdot(vst_ref[g, :, pl.ds(off, KV_TILE)], p)))
        return tuple(out)

    front, back = (sa_ref, sb_ref), (sc2_ref, sd_ref)
    for g in groups:
        sa_ref[g] = sa_ref[g] + tile_bias(g, 0)
        sb_ref[g] = sb_ref[g] + tile_bias(g, 1)

    def pair(k, cur, nxt, carry):
        scores(k + 2, nxt[0])
        carry = consume(k, cur[0], carry)
        scores(k + 3, nxt[1])
        return consume(k + 1, cur[1], carry)

    n_before = (qi * Q_BLOCK) // KV_TILE

    def last_pair(k, cur, carry):
        carry = consume(k, cur[0], carry, True)
        return lax.cond(n_before % 2 == 1, lambda c: consume(k + 1, cur[1], c, True), lambda c: c, carry)

    init = tuple((jnp.full((1, width), NEG_INF, F32), jnp.zeros((vrows, width), F32)) for _ in groups)
    n_pairs = n_before // 2
    carry = lax.fori_loop(0, n_pairs // 2,
                          lambda i, c: pair(4 * i + 2, back, front, pair(4 * i, front, back, c)), init)
    k0 = 4 * (n_pairs // 2)
    carry = lax.cond(n_pairs % 2 == 1,
                     lambda c: last_pair(k0 + 2, back, pair(k0, front, back, c)),
                     lambda c: last_pair(k0, front, c), carry)

    for g in groups:
        acc = carry[g][1]
        o_slc = acc[0:dh, :] / jnp.maximum(acc[dh:dh + 1, :], 1e-30)
        gate = jax.nn.sigmoid(g_ref[0, g])
        for h in range(NSA_HPG):
            hs = slice(h * Q_BLOCK, (h + 1) * Q_BLOCK)
            y = (gate[3 * h:3 * h + 1, :] * o_cmp[g][:, hs] + gate[3 * h + 1:3 * h + 2, :] * o_slc[:, hs]
                 + gate[3 * h + 2:3 * h + 3, :] * o_win[g][:, hs])
            row = (g * NSA_HPG + h) * dh
            out_ref[row:row + dh, :] = y.astype(out_ref.dtype)


def _nsa_pair(qt, kc, vct, ks, vst, kw, vwt, gates):
    g, nqt, dh, width = qt.shape
    seq = ks.shape[0]
    nc = kc.shape[0]
    ns = seq // SLC_BLOCK
    resident = lambda a: pl.BlockSpec(a.shape, lambda qi: (0,) * a.ndim, pipeline_mode=pl.Buffered(1))
    return pl.pallas_call(
        functools.partial(_nsa_pair_kernel, nc=nc, ns=ns),
        out_shape=jax.ShapeDtypeStruct((g * NSA_HPG * dh, seq), BF16),
        grid=(nqt,),
        in_specs=[pl.BlockSpec((g, 1, dh, width), lambda qi: (0, qi, 0, 0)),
                  resident(kc), resident(vct), resident(ks), resident(vst), resident(kw), resident(vwt),
                  pl.BlockSpec((1,) + gates.shape[1:], lambda qi: (qi, 0, 0, 0))],
        out_specs=pl.BlockSpec((g * NSA_HPG * dh, Q_BLOCK), lambda qi: (0, qi)),
        scratch_shapes=[pltpu.VMEM((g, nc, width), F32),
                        pltpu.VMEM((g, nc + 2 * SUBLANES, Q_BLOCK), F32),
                        pltpu.VMEM((g, ns, Q_BLOCK), F32),
                        ] + [pltpu.VMEM((g, KV_TILE, width), F32)] * 4,
        compiler_params=_cparams(("arbitrary",)),
        name="sparse_attention",
    )(qt, kc, vct, ks, vst, kw, vwt, gates)


def _split_bf16(x):
    hi = x.astype(BF16)
    return hi, (x - hi.astype(F32)).astype(BF16)


def _gla_kernel(h_ref, wa_ref, ba_ref, ng_ref, y_ref, st_ref):
    @pl.when(pl.program_id(0) == 0)
    def _():
        st_ref[...] = jnp.zeros_like(st_ref)

    kw = GLA_HEADS * GLA_DK
    vw = GLA_HEADS * GLA_DV
    c = GLA_CHUNK
    sb = min(GLA_SUPER, h_ref.shape[0])
    ri = lax.broadcasted_iota(I32, (sb, sb), 0)
    cj = lax.broadcasted_iota(I32, (sb, sb), 1)
    same_chunk = (ri // c) == (cj // c)
    tril = same_chunk & (ri >= cj)
    tril_m = jnp.where(tril, 1.0, 0.0).astype(BF16)
    chunk_m = jnp.where(same_chunk, 1.0, 0.0).astype(BF16)
    ones_c = jnp.ones((c, LANES), BF16)
    krow = lax.broadcasted_iota(I32, (kw, vw), 0) // GLA_DK
    vcol = lax.broadcasted_iota(I32, (kw, vw), 1) // GLA_DV
    blockdiag = krow == vcol
    klane = lax.broadcasted_iota(I32, (1, kw), 1) // GLA_DK
    vlane = lax.broadcasted_iota(I32, (1, vw), 1) // GLA_DV
    va = lax.broadcasted_iota(I32, (vw, vw), 0) // GLA_DV
    vb = lax.broadcasted_iota(I32, (vw, vw), 1) // GLA_DV
    head_avg = jnp.where(va == vb, 1.0 / GLA_DV, 0.0).astype(BF16)
    tn = (((0,), (0,)), ((), ()))

    state = st_ref[...]
    for blk in range(h_ref.shape[0] // sb):
        rows = slice(blk * sb, (blk + 1) * sb)
        q = h_ref[rows, 0:kw] * (GLA_DK ** -0.5)
        k = h_ref[rows, kw:2 * kw]
        v = h_ref[rows, 2 * kw:2 * kw + vw].astype(BF16)
        r = h_ref[rows, 2 * kw + vw:2 * kw + 2 * vw]
        a_low = h_ref[rows, 2 * kw + 2 * vw:2 * kw + 2 * vw + LANES]
        la = jax.nn.log_sigmoid(_hdot(a_low, wa_ref[...]) + ba_ref[...]) / GLA_TAU
        la_hi, la_lo = _split_bf16(la)
        sum_la = lambda mat: (jnp.dot(mat, la_hi, preferred_element_type=F32)
                              + jnp.dot(mat, la_lo, preferred_element_type=F32))
        bcum = sum_la(tril_m)
        b_last = sum_la(chunk_m)
        q_t = (q * jnp.exp(bcum)).astype(BF16)
        k_t = (k * jnp.exp(-bcum)).astype(BF16)
        k_d = (k * jnp.exp(b_last - bcum)).astype(BF16)
        o = jnp.zeros((sb, vw), F32)
        for hd in range(GLA_HEADS):
            attn = _bdot_nt(jnp.where(klane == hd, q_t, jnp.zeros_like(q_t)), k_t)
            attn = jnp.where(tril, attn, 0.0).astype(BF16)
            o = o + jnp.where(vlane == hd, jnp.dot(attn, v, preferred_element_type=F32), 0.0)
        inter = []
        for n in range(sb // c):
            cs = slice(n * c, (n + 1) * c)
            inter.append(jnp.dot(q_t[cs], state.astype(BF16), preferred_element_type=F32))
            kv = jnp.where(blockdiag, lax.dot_general(k_d[cs], v[cs], tn, preferred_element_type=F32), 0.0)
            tot = (lax.dot_general(la_hi[cs], ones_c, tn, preferred_element_type=F32)
                   + lax.dot_general(la_lo[cs], ones_c, tn, preferred_element_type=F32))
            decay = jnp.exp(tot)
            state = jnp.concatenate([decay] * (vw // LANES), axis=1) * state + kv
        o = o + jnp.concatenate(inter, axis=0)
        sq_hi, sq_lo = _split_bf16(o * o)
        ms = (jnp.dot(sq_hi, head_avg, preferred_element_type=F32)
              + jnp.dot(sq_lo, head_avg, preferred_element_type=F32))
        o = o * lax.rsqrt(ms + EPS)
        y_ref[rows, :] = (o * ng_ref[...] * (r * jax.nn.sigmoid(r))).astype(y_ref.dtype)
    st_ref[...] = state


def _gla_mixer(hd, w_a2p, b_a, norm_g, l):
    seq, wid = hd.shape
    tb = min(seq, 512)
    kw = GLA_HEADS * GLA_DK
    vw = GLA_HEADS * GLA_DV
    return pl.pallas_call(
        _gla_kernel,
        out_shape=jax.ShapeDtypeStruct((seq, vw), BF16),
        grid=(seq // tb,),
        in_specs=[pl.BlockSpec((tb, wid), lambda i: (i, 0)), _of_layer(w_a2p, l), _of_layer(b_a, l),
                  _of_layer(norm_g, l)],
        out_specs=pl.BlockSpec((tb, vw), lambda i: (i, 0)),
        scratch_shapes=[pltpu.VMEM((kw, vw), F32)],
        compiler_params=_cparams(("arbitrary",)),
        name="gla_mixer",
    )(hd, w_a2p, b_a, norm_g)


def _merge_kernel(x_ref, g_ref, ys_ref, yn_ref, yg_ref, wgm_ref, ws_ref, wn_ref, wg_ref, wo_ref, o_ref):
    x = x_ref[...]
    d = x.shape[1]
    xn = _rms(x, g_ref[...]).astype(BF16)
    gm = jax.nn.sigmoid(jnp.dot(xn, wgm_ref[...], preferred_element_type=F32))
    mixed = (gm[:, 0:d] * jnp.dot(ys_ref[...], ws_ref[...], preferred_element_type=F32)
             + gm[:, d:2 * d] * lax.dot_general(yn_ref[...], wn_ref[...], (((0,), (0,)), ((), ())),
                                                preferred_element_type=F32)
             + gm[:, 2 * d:3 * d] * jnp.dot(yg_ref[...], wg_ref[...], preferred_element_type=F32))
    o_ref[...] = x + _bdot(mixed, wo_ref[...])


def _merge(x, g, ys, yn, yg, wgm, ws, wn, wg, wo, l):
    seq, d = x.shape
    tm = min(seq, 512)
    row = lambda a: pl.BlockSpec((tm, a.shape[1]), lambda i: (i, 0))
    return pl.pallas_call(
        _merge_kernel,
        out_shape=jax.ShapeDtypeStruct((seq, d), F32),
        grid=(seq // tm,),
        in_specs=[row(x), _of_layer(g, l), row(ys), pl.BlockSpec((yn.shape[0], tm), lambda i: (0, i)), row(yg)]
                 + [_of_layer(a, l) for a in (wgm, ws, wn, wg, wo)],
        out_specs=row(x),
        compiler_params=_cparams(("parallel",)),
        name="merge",
    )(x, g, ys, yn, yg, wgm, ws, wn, wg, wo)


def _ffn_kernel(x_ref, xp_ref, g_ref, wug_ref, wuv_ref, cwg_ref, cwv_ref, cbg_ref, cbv_ref, wd_ref,
                fg_ref, o_ref, xn_ref, *, final):
    i = pl.program_id(0)
    j = pl.program_id(1)
    tm = x_ref.shape[0]
    halo = xp_ref.shape[0]

    @pl.when(j == 0)
    def _():
        xn_ref[0:halo, :] = jnp.where(i == 0, 0.0, _rms(xp_ref[...], g_ref[...])).astype(BF16)
        xn_ref[halo:halo + tm, :] = _rms(x_ref[...], g_ref[...]).astype(BF16)

    xn = xn_ref[...]

    def conv(w_ref, cw_ref, cb_ref):
        h = jnp.dot(xn, w_ref[...], preferred_element_type=F32)
        hc = cb_ref[...]
        for t in range(CONV_WIDTH):
            sh = CONV_WIDTH - 1 - t
            hs = h if sh == 0 else pltpu.roll(h, sh, 0)
            hc = hc + cw_ref[t:t + 1, :] * hs[halo:halo + tm, :]
        return hc

    act = jax.nn.gelu(conv(wug_ref, cwg_ref, cbg_ref)) * conv(wuv_ref, cwv_ref, cbv_ref)
    part = _bdot(act, wd_ref[...])

    @pl.when(j == 0)
    def _():
        o_ref[...] = x_ref[...] + part

    @pl.when(j > 0)
    def _():
        o_ref[...] = o_ref[...] + part

    if final:
        @pl.when(j == pl.num_programs(1) - 1)
        def _():
            o_ref[...] = _rms(o_ref[...], fg_ref[...])


def _ffn(x, g, w_up, conv_w, conv_b, w_down, final_g, final, l):
    seq, d = x.shape
    dff = w_down.shape[1]
    tm = min(seq, 1024)
    nj = 2
    tn = dff // nj
    halo = SUBLANES
    hb = tm // halo
    return pl.pallas_call(
        functools.partial(_ffn_kernel, final=final),
        out_shape=jax.ShapeDtypeStruct((seq, d), F32),
        grid=(seq // tm, nj),
        in_specs=[pl.BlockSpec((tm, d), lambda i, j: (i, 0)),
                  pl.BlockSpec((halo, d), lambda i, j: (jnp.maximum(i * hb - 1, 0), 0)),
                  _of_layer(g, l),
                  pl.BlockSpec((None, d, tn), lambda i, j: (l, 0, j)),
                  pl.BlockSpec((None, d, tn), lambda i, j: (l, 0, nj + j)),
                  pl.BlockSpec((None, CONV_WIDTH, tn), lambda i, j: (l, 0, j)),
                  pl.BlockSpec((None, CONV_WIDTH, tn), lambda i, j: (l, 0, nj + j)),
                  pl.BlockSpec((None, 1, tn), lambda i, j: (l, 0, j)),
                  pl.BlockSpec((None, 1, tn), lambda i, j: (l, 0, nj + j)),
                  pl.BlockSpec((None, tn, d), lambda i, j: (l, j, 0)),
                  pl.BlockSpec((1, d), lambda i, j: (0, 0))],
        out_specs=pl.BlockSpec((tm, d), lambda i, j: (i, 0)),
        scratch_shapes=[pltpu.VMEM((halo + tm, d), BF16)],
        compiler_params=_cparams(("parallel", "arbitrary")),
        name="conv_ffn",
    )(x, x, g, w_up, w_up, conv_w, conv_w, conv_b, conv_b, w_down, final_g)


def _split_w_in(w):
    depth, d = w.shape[:2]
    s5w = d // 4
    nsa_w = NSA_HEADS * HEAD_DIM
    kvw = NSA_KV_GROUPS * HEAD_DIM
    kw = GLA_HEADS * GLA_DK
    vw = GLA_HEADS * GLA_DV
    sizes = (s5w, nsa_w, kvw, kvw, kvw, kvw, kvw, kvw, 3 * NSA_HEADS, kw, kw, vw, 16, vw, 3 * d)
    assert sum(sizes) == w.shape[2]
    offs = [0]
    for sz in sizes:
        offs.append(offs[-1] + sz)
    col = lambda n: w[:, :, offs[n]:offs[n + 1]]
    (u, q, kc, vc, ks, vs, kwn, vwn, gn, gq, gk, gv, ga, gr, gm) = [col(n) for n in range(len(sizes))]
    padl = lambda a: jnp.pad(a, ((0, 0), (0, 0), (0, LANES - a.shape[2])))
    tr = lambda a: jnp.swapaxes(a, 1, 2)
    per_group = 3 * NSA_HPG
    gates_t = jnp.pad(tr(gn).reshape(depth, NSA_KV_GROUPS, per_group, d),
                      ((0, 0), (0, 0), (0, BF16_SUBLANES - per_group), (0, 0))).reshape(depth, -1, d)
    out = dict(u=u, k=jnp.concatenate([kc, ks, kwn], axis=2), vc=vc,
               d=jnp.concatenate([gq, gk, gv, gr, padl(ga)], axis=2),
               qt=tr(q), vt=tr(jnp.concatenate([vs, vwn], axis=2)), gt=gates_t, gm=gm)
    return {name: a.astype(BF16) for name, a in out.items()}


def kernel(x, positions, norm1_g, w_in, s5_lam_re, s5_lam_im, s5_log_dt, s5_b_re, s5_b_im, s5_c_re, s5_c_im, s5_d, s5_w_glu, s5_b_glu, nsa_pe_k, nsa_pe_v, nsa_ck_w1, nsa_ck_w2, nsa_cv_w1, nsa_cv_w2, gla_w_a2, gla_b_a, gla_norm_g, w_br_s5, w_br_nsa, w_br_gla, w_out, norm2_g, ffn_w_up, ffn_conv_w, ffn_conv_b, ffn_w_down, final_g):
    bsz, seq, d = x.shape
    depth = w_in.shape[0]
    assert bsz == 1 and seq % KV_TILE == 0 and seq >= 2 * KV_TILE and d % LANES == 0
    g = NSA_KV_GROUPS
    dh = HEAD_DIM
    rows = lambda a: a.reshape(depth, 1, -1)
    bf = lambda a: a.astype(BF16)

    w = _split_w_in(w_in)
    g1 = rows(norm1_g)
    pe = jnp.stack([nsa_pe_k.reshape(depth, 1, -1), nsa_pe_v.reshape(depth, 1, -1)], axis=1)
    pe = jnp.broadcast_to(pe, (depth, 2, SUBLANES, pe.shape[-1]))
    w1 = bf(jnp.stack([nsa_ck_w1, nsa_cv_w1], axis=1))
    taps = w1.reshape(depth, 2, 1, CMP_BLOCK, dh, w1.shape[-1])
    w1g = jnp.concatenate(
        [jnp.pad(taps, ((0, 0),) * 4 + ((gi * dh, (g - 1 - gi) * dh), (0, 0))) for gi in range(g)], axis=2)
    w2k = bf(jnp.stack([jnp.pad(nsa_ck_w2, ((0, 0), (0, 0), (gi * dh, (g - 1 - gi) * dh))) for gi in range(g)],
                       axis=1))
    w2v = bf(jnp.swapaxes(nsa_cv_w2, 1, 2))
    w_a2p = jnp.pad(gla_w_a2, ((0, 0), (0, LANES - gla_w_a2.shape[1]), (0, 0)))
    w_glu, wb_s5, wb_nsa, wb_gla, wo = bf(s5_w_glu), bf(w_br_s5), bf(w_br_nsa), bf(w_br_gla), bf(w_out)
    w_up, w_down = bf(ffn_w_up), bf(ffn_w_down)

    xs = x.reshape(seq, d)
    tabs = _rope_tables(positions.reshape(seq))

    for l in range(depth):
        u_s5, qt, x_cmp, ks, kwn, vst, vwt, gates, hd = _project(xs, g1, w, tabs, l)

        prep = _s5_prep(s5_lam_re[l], s5_lam_im[l], s5_log_dt[l], s5_b_re[l], s5_b_im[l],
                        s5_c_re[l], s5_c_im[l])
        y_s5 = _s5_mixer(u_s5, prep, rows(s5_d), w_glu, rows(s5_b_glu), l)

        kc, vct = _compress(x_cmp, w1g, w1, w2k, w2v, pe, l)
        y_nsa = _nsa_pair(qt, kc, vct, ks, vst, kwn, vwt, gates)

        y_gla = _gla_mixer(hd, w_a2p, rows(gla_b_a), rows(gla_norm_g), l)

        xs = _merge(xs, g1, y_s5, y_nsa, y_gla, w["gm"], wb_s5, wb_nsa, wb_gla, wo, l)
        xs = _ffn(xs, rows(norm2_g), w_up, ffn_conv_w, rows(ffn_conv_b), w_down, final_g.reshape(1, -1),
                  final=(l == depth - 1), l=l)
    return xs.reshape(bsz, seq, d)
```

```python
import functools
import math

import jax
import jax.numpy as jnp
from jax import lax
from jax.experimental import pallas as pl
from jax.experimental.pallas import tpu as pltpu

F32 = jnp.float32
BF16 = jnp.bfloat16
I32 = jnp.int32

EPS = 1e-6
NEG_INF = -1e30

S5_GROUP = 16
S5_STATE = 64
NSA_HEADS = 8
NSA_KV_GROUPS = 2
NSA_HPG = NSA_HEADS // NSA_KV_GROUPS
HEAD_DIM = 64
ROPE_DIM = 16
ROPE_THETA = 500000.0
CMP_BLOCK = 32
CMP_STRIDE = 16
SLC_BLOCK = 64
SLC_TOPK = 16
WINDOW = 512
Q_BLOCK = 128
FORCE_BONUS = 1.0e4
GLA_HEADS = 4
GLA_DK = 32
GLA_DV = 64
GLA_TAU = 16.0
GLA_CHUNK = 64
GLA_SUPER = 256
CONV_WIDTH = 3

LANES = 128
SUBLANES = 8
KV_TILE = 512
CMP_CHUNK = 256
BF16_SUBLANES = 16
VMEM_LIMIT = 56 * 1024 * 1024


def _cparams(sem):
    return pltpu.CompilerParams(dimension_semantics=sem, vmem_limit_bytes=VMEM_LIMIT)


def _of_layer(a, l):
    return pl.BlockSpec((None,) + a.shape[1:], lambda *_: (l,) + (0,) * (a.ndim - 1))


def _bdot(a, b):
    return jnp.dot(a.astype(BF16), b.astype(BF16), preferred_element_type=F32)


def _bdot_nt(a, b):
    return lax.dot_general(a.astype(BF16), b.astype(BF16), (((1,), (1,)), ((), ())),
                           preferred_element_type=F32)


def _hdot(a, b):
    return jnp.dot(a, b, preferred_element_type=F32, precision=lax.Precision.HIGHEST)


def _rms(x, g):
    return x * lax.rsqrt(jnp.mean(x * x, axis=-1, keepdims=True) + EPS) * g


def _rope_tab_kernel(pos_ref, post_ref, invf_ref, invft_ref, c_ref, s1_ref, s2_ref, ct_ref, s1t_ref, s2t_ref):
    half = ROPE_DIM // 2

    def tables(ang, dim, c_out, s1_out, s2_out):
        cosv = jnp.cos(ang)
        sinv = jnp.sin(ang)
        c_out[...] = jnp.where(dim < ROPE_DIM, cosv, 1.0)
        s1_out[...] = jnp.where(dim < half, 0.0, jnp.where(dim < ROPE_DIM, sinv, 0.0))
        s2_out[...] = jnp.where(dim < half, -sinv, 0.0)

    ang = pos_ref[...].astype(F32) * invf_ref[...]
    tables(ang, lax.broadcasted_iota(I32, ang.shape, 1) % HEAD_DIM, c_ref, s1_ref, s2_ref)
    ang_t = invft_ref[...] * post_ref[...].astype(F32)
    tables(ang_t, lax.broadcasted_iota(I32, ang_t.shape, 0), ct_ref, s1t_ref, s2t_ref)


def _rope_tables(positions):
    seq = positions.shape[0]
    tm = min(seq, 2048)
    inv_freq = ROPE_THETA ** (-jnp.arange(0, ROPE_DIM, 2, dtype=F32) / ROPE_DIM)
    dim = jnp.arange(LANES) % HEAD_DIM
    invf = jnp.where(dim < ROPE_DIM, inv_freq[dim % (ROPE_DIM // 2)], 0.0)
    tab = jax.ShapeDtypeStruct((seq, LANES), F32)
    tab_t = jax.ShapeDtypeStruct((HEAD_DIM, seq), F32)
    return pl.pallas_call(
        _rope_tab_kernel,
        out_shape=(tab, tab, tab, tab_t, tab_t, tab_t),
        grid=(seq // tm,),
        in_specs=[pl.BlockSpec((tm, 1), lambda i: (i, 0)),
                  pl.BlockSpec((1, tm), lambda i: (0, i)),
                  pl.BlockSpec((1, LANES), lambda i: (0, 0)),
                  pl.BlockSpec((HEAD_DIM, 1), lambda i: (0, 0))],
        out_specs=tuple([pl.BlockSpec((tm, LANES), lambda i: (i, 0))] * 3
                        + [pl.BlockSpec((HEAD_DIM, tm), lambda i: (0, i))] * 3),
        compiler_params=_cparams(("parallel",)),
        name="rope_tables",
    )(positions.reshape(seq, 1), positions.reshape(1, seq), invf.reshape(1, LANES),
      invf[:HEAD_DIM].reshape(HEAD_DIM, 1))


def _proj_kernel(x_ref, g_ref, wu_ref, wk_ref, wvc_ref, wd_ref, wqt_ref, wvt_ref, wgt_ref,
                 c_ref, s1_ref, s2_ref, ct_ref, s1t_ref, s2t_ref,
                 u_ref, qt_ref, cmp_ref, ks_ref, kw_ref, vst_ref, vwt_ref, gt_ref, hd_ref):
    xn = _rms(x_ref[...], g_ref[...]).astype(BF16)
    tm = xn.shape[0]
    n_qt = tm // Q_BLOCK
    half = ROPE_DIM // 2
    fdot = lambda a, b: jnp.dot(a, b, preferred_element_type=F32)
    tdot = lambda w, a: lax.dot_general(w, a, (((1,), (1,)), ((), ())), preferred_element_type=F32)

    u_ref[...] = fdot(xn, wu_ref[...])
    hd_ref[...] = fdot(xn, wd_ref[...])
    cmp_ref[1] = fdot(xn, wvc_ref[...])
    hk = fdot(xn, wk_ref[...])
    c, s1, s2 = c_ref[...], s1_ref[...], s2_ref[...]
    for j, out in enumerate((cmp_ref.at[0], ks_ref, kw_ref)):
        piece = hk[:, j * LANES:(j + 1) * LANES]
        rot = piece * c + pltpu.roll(piece, half, 1) * s1 + pltpu.roll(piece, LANES - half, 1) * s2
        out[...] = rot.astype(out.dtype)

    q_t = tdot(wqt_ref[...], xn)
    ct, s1t, s2t = ct_ref[...], s1t_ref[...], s2t_ref[...]
    dh = ct.shape[0]
    for gh in range(q_t.shape[0] // dh):
        blk = q_t[gh * dh:(gh + 1) * dh, :]
        rot = blk * ct + pltpu.roll(blk, half, 0) * s1t + pltpu.roll(blk, dh - half, 0) * s2t
        rot = (rot * (dh ** -0.5 * math.log2(math.e))).astype(qt_ref.dtype)
        g, h = divmod(gh, NSA_HPG)
        for j in range(n_qt):
            qt_ref[g, j, :, h * Q_BLOCK:(h + 1) * Q_BLOCK] = rot[:, j * Q_BLOCK:(j + 1) * Q_BLOCK]

    v_t = tdot(wvt_ref[...], xn)
    rid = lax.broadcasted_iota(I32, (vst_ref.shape[1] - dh, tm), 0)
    tail = jnp.where(rid == 0, 1.0, 0.0).astype(vst_ref.dtype)
    for b, out in enumerate((vst_ref, vwt_ref)):
        for g in range(out.shape[0]):
            row = (b * out.shape[0] + g) * dh
            out[g, 0:dh, :] = v_t[row:row + dh, :].astype(out.dtype)
            out[g, dh:, :] = tail

    g_t = tdot(wgt_ref[...], xn)
    rows = gt_ref.shape[2]
    for j in range(n_qt):
        for g in range(gt_ref.shape[1]):
            gt_ref[j, g] = g_t[g * rows:(g + 1) * rows, j * Q_BLOCK:(j + 1) * Q_BLOCK]


def _project(x, g, w, tabs, l):
    seq, d = x.shape
    tm = min(seq, 1024)
    n_qt = tm // Q_BLOCK
    groups = NSA_KV_GROUPS
    vrows = HEAD_DIM + BF16_SUBLANES
    gate_rows = w["gt"].shape[1] // groups
    row = lambda wid: pl.BlockSpec((tm, wid), lambda i: (i, 0))
    col = lambda r: pl.BlockSpec((r, tm), lambda i: (0, i))
    weights = [w[k] for k in ("u", "k", "vc", "d", "qt", "vt", "gt")]
    sds = jax.ShapeDtypeStruct
    return pl.pallas_call(
        _proj_kernel,
        out_shape=(sds((seq, w["u"].shape[2]), F32),
                   sds((groups, seq // Q_BLOCK, HEAD_DIM, NSA_HPG * Q_BLOCK), BF16),
                   sds((2, seq, LANES), F32),
                   sds((seq, LANES), BF16), sds((seq, LANES), BF16),
                   sds((groups, vrows, seq), BF16), sds((groups, vrows, seq), BF16),
                   sds((seq // Q_BLOCK, groups, gate_rows, Q_BLOCK), F32),
                   sds((seq, w["d"].shape[2]), F32)),
        grid=(seq // tm,),
        in_specs=[row(d), _of_layer(g, l)] + [_of_layer(a, l) for a in weights]
                 + [row(LANES)] * 3 + [col(HEAD_DIM)] * 3,
        out_specs=(row(w["u"].shape[2]),
                   pl.BlockSpec((groups, n_qt, HEAD_DIM, NSA_HPG * Q_BLOCK), lambda i: (0, i, 0, 0)),
                   pl.BlockSpec((2, tm, LANES), lambda i: (0, i, 0)), row(LANES), row(LANES),
                   pl.BlockSpec((groups, vrows, tm), lambda i: (0, 0, i)),
                   pl.BlockSpec((groups, vrows, tm), lambda i: (0, 0, i)),
                   pl.BlockSpec((n_qt, groups, gate_rows, Q_BLOCK), lambda i: (i, 0, 0, 0)),
                   row(w["d"].shape[2])),
        compiler_params=_cparams(("parallel",)),
        name="in_proj",
    )(x, g, *weights, *tabs)


def _s5_prep_kernel(lr_ref, li_ref, ldt_ref, btr_ref, bti_ref, ctr_ref, cti_ref,
                    bre_ref, bim_ref, cre_ref, cim_ref, tab_ref):
    lr = lr_ref[...]
    li = li_ref[...]
    dt = jnp.exp(ldt_ref[...])
    mag = jnp.exp(lr * dt)
    ar = mag * jnp.cos(li * dt)
    ai = mag * jnp.sin(li * dt)
    nr = ar - 1.0
    ni = ai
    den = lr * lr + li * li
    f_re = (nr * lr + ni * li) / den
    f_im = (ni * lr - nr * li) / den
    btr = btr_ref[...]
    bti = bti_ref[...]
    rows = lax.broadcasted_iota(I32, btr.shape, 0) // S5_GROUP
    cols = lax.broadcasted_iota(I32, btr.shape, 1) // S5_STATE
    diag = rows == cols
    bre_ref[...] = jnp.where(diag, f_re * btr - f_im * bti, 0.0).astype(BF16)
    bim_ref[...] = jnp.where(diag, f_re * bti + f_im * btr, 0.0).astype(BF16)
    cre_ref[...] = jnp.where(diag, ctr_ref[...], 0.0).astype(BF16)
    cim_ref[...] = jnp.where(diag, cti_ref[...], 0.0).astype(BF16)
    pr = [ar]
    pi = [ai]
    for _ in range(SUBLANES - 1):
        pr_n = pr[-1] * ar - pi[-1] * ai
        pi_n = pr[-1] * ai + pi[-1] * ar
        pr.append(pr_n)
        pi.append(pi_n)
    rid = lax.broadcasted_iota(I32, (SUBLANES, lr.shape[1]), 0)
    zero = jnp.zeros((SUBLANES, lr.shape[1]), F32)
    p_re = zero
    p_im = zero
    for k in range(SUBLANES):
        p_re = jnp.where(rid == k, pr[k], p_re)
        p_im = jnp.where(rid == k, pi[k], p_im)
    for n, k in enumerate((1, 2, 4)):
        tab_ref[2 * n] = jnp.where(rid >= k, pr[k - 1], 0.0)
        tab_ref[2 * n + 1] = jnp.where(rid >= k, pi[k - 1], 0.0)
    tab_ref[6] = p_re
    tab_ref[7] = p_im


def _s5_prep(lam_re, lam_im, log_dt, b_re, b_im, c_re, c_im):
    g, p = lam_re.shape
    h = b_re.shape[-1]
    gp = g * p
    rowv = lambda a: a.reshape(1, gp)
    tile_t = lambda a: jnp.tile(a.reshape(h, gp), (g, 1))
    btr = tile_t(jnp.transpose(b_re, (2, 0, 1)))
    bti = tile_t(jnp.transpose(b_im, (2, 0, 1)))
    ctr = tile_t(jnp.transpose(c_re, (1, 0, 2)))
    cti = tile_t(jnp.transpose(c_im, (1, 0, 2)))
    mat = jax.ShapeDtypeStruct((g * h, gp), BF16)
    return pl.pallas_call(
        _s5_prep_kernel,
        out_shape=(mat, mat, mat, mat, jax.ShapeDtypeStruct((8, SUBLANES, gp), F32)),
        name="s5_prep",
    )(rowv(lam_re), rowv(lam_im), rowv(jnp.repeat(log_dt, p)), btr, bti, ctr, cti)


def _s5_kernel(u_ref, bre_ref, bim_ref, cre_ref, cim_ref, tab_ref, d_ref, wg_ref, bg_ref,
               y_ref, xr_ref, xi_ref, car_ref):
    @pl.when(pl.program_id(0) == 0)
    def _():
        car_ref[...] = jnp.zeros_like(car_ref)

    u = u_ref[...]
    ub = u.astype(BF16)
    xr_ref[...] = jnp.dot(ub, bre_ref[...], preferred_element_type=F32)
    xi_ref[...] = jnp.dot(ub, bim_ref[...], preferred_element_type=F32)
    tb = u.shape[0]

    def slab(r, carry):
        cr, ci = carry
        off = pl.multiple_of(r * SUBLANES, SUBLANES)
        xr = xr_ref[pl.ds(off, SUBLANES), :]
        xi = xi_ref[pl.ds(off, SUBLANES), :]
        for n, k in enumerate((1, 2, 4)):
            tr = tab_ref[2 * n]
            ti = tab_ref[2 * n + 1]
            sr = pltpu.roll(xr, k, 0)
            si = pltpu.roll(xi, k, 0)
            xr, xi = xr + tr * sr - ti * si, xi + tr * si + ti * sr
        pr = tab_ref[6]
        pi = tab_ref[7]
        xr, xi = xr + pr * cr - pi * ci, xi + pr * ci + pi * cr
        xr_ref[pl.ds(off, SUBLANES), :] = xr
        xi_ref[pl.ds(off, SUBLANES), :] = xi
        return xr[SUBLANES - 1:SUBLANES, :], xi[SUBLANES - 1:SUBLANES, :]

    cr, ci = lax.fori_loop(0, tb // SUBLANES, slab, (car_ref[0:1, :], car_ref[1:2, :]))
    car_ref[0:1, :] = cr
    car_ref[1:2, :] = ci
    y = _bdot_nt(xr_ref[...], cre_ref[...]) - _bdot_nt(xi_ref[...], cim_ref[...])
    y = y + d_ref[...] * u
    y = jax.nn.gelu(y)
    y = y * jax.nn.sigmoid(_bdot(y, wg_ref[...]) + bg_ref[...])
    y_ref[...] = y.astype(y_ref.dtype)


def _s5_mixer(u, prep, d_skip, w_glu, b_glu, l):
    seq, w = u.shape
    bre, bim, cre, cim, tab = prep
    gp = bre.shape[1]
    tb = min(seq, 512)
    full = lambda a: pl.BlockSpec(a.shape, lambda i: (0,) * a.ndim)
    return pl.pallas_call(
        _s5_kernel,
        out_shape=jax.ShapeDtypeStruct((seq, w), BF16),
        grid=(seq // tb,),
        in_specs=[pl.BlockSpec((tb, w), lambda i: (i, 0)), full(bre), full(bim), full(cre),
                  full(cim), full(tab), _of_layer(d_skip, l), _of_layer(w_glu, l), _of_layer(b_glu, l)],
        out_specs=pl.BlockSpec((tb, w), lambda i: (i, 0)),
        scratch_shapes=[pltpu.VMEM((tb, gp), F32), pltpu.VMEM((tb, gp), F32),
                        pltpu.VMEM((SUBLANES, gp), F32)],
        compiler_params=_cparams(("arbitrary",)),
        name="s5_mixer",
    )(u, bre, bim, cre, cim, tab, d_skip, w_glu, b_glu)


def _compress_kernel(x_ref, w1g_ref, w1_ref, w2k_ref, w2v_ref, pe_ref, kc_ref, vct_ref):
    seq = x_ref.shape[1]
    nc = seq // CMP_STRIDE
    groups = w1g_ref.shape[1]
    fdot = lambda a, b: jnp.dot(a, b, preferred_element_type=F32)

    def hidden(kind):
        taps = [x_ref[kind, pl.ds(l, nc, stride=CMP_STRIDE), :].astype(BF16) for l in range(CMP_STRIDE)]
        pew = _bdot(pe_ref[kind], w1_ref[kind])[0:1, :]
        out = []
        for g in range(groups):
            a = fdot(taps[0], w1g_ref[kind, g, 0])
            b = fdot(taps[0], w1g_ref[kind, g, CMP_STRIDE])
            for l in range(1, CMP_STRIDE):
                a = a + fdot(taps[l], w1g_ref[kind, g, l])
                b = b + fdot(taps[l], w1g_ref[kind, g, CMP_STRIDE + l])
            out.append(jax.nn.gelu(a + pltpu.roll(b, nc - 1, 0) + pew).astype(BF16))
        return out

    hk = hidden(0)
    kc = fdot(hk[0], w2k_ref[0])
    for g in range(1, groups):
        kc = kc + fdot(hk[g], w2k_ref[g])
    kc_ref[...] = kc.astype(kc_ref.dtype)

    hv = hidden(1)
    dh = w2v_ref.shape[0]
    rid = lax.broadcasted_iota(I32, (vct_ref.shape[1] - dh, nc), 0)
    tail = jnp.where(rid == 0, 1.0, 0.0).astype(vct_ref.dtype)
    for g in range(groups):
        vct_ref[g, 0:dh, :] = lax.dot_general(w2v_ref[...], hv[g], (((1,), (1,)), ((), ())),
                                              preferred_element_type=F32).astype(vct_ref.dtype)
        vct_ref[g, dh:, :] = tail


def _compress(x, w1g, w1, w2k, w2v, pe, l):
    seq = x.shape[1]
    nc = seq // CMP_STRIDE
    groups = w1g.shape[2]
    dh = w2v.shape[1]
    out_shape = (jax.ShapeDtypeStruct((nc, groups * dh), BF16),
                 jax.ShapeDtypeStruct((groups, dh + BF16_SUBLANES, nc), BF16))
    whole = lambda a, **kw: pl.BlockSpec(a.shape, lambda i: (0,) * len(a.shape), **kw)
    once = dict(pipeline_mode=pl.Buffered(1))
    layer = lambda a: pl.BlockSpec((None,) + a.shape[1:], lambda i: (l,) + (0,) * (a.ndim - 1), **once)
    return pl.pallas_call(
        _compress_kernel,
        out_shape=out_shape,
        grid=(1,),
        in_specs=[whole(x, **once)] + [layer(a) for a in (w1g, w1, w2k, w2v, pe)],
        out_specs=tuple(whole(o) for o in out_shape),
        compiler_params=_cparams(("arbitrary",)),
        name="compress",
    )(x, w1g, w1, w2k, w2v, pe)


def _nsa_pair_kernel(qt_ref, kc_ref, vct_ref, ks_ref, vst_ref, kw_ref, vwt_ref, g_ref,
                     out_ref, sc_ref, imp_ref, selb_ref, sa_ref, sb_ref, sc2_ref, sd_ref, *, nc, ns):
    qi = pl.program_id(0)
    groups = range(qt_ref.shape[0])
    dh, width = qt_ref.shape[2:]
    zeros = jnp.zeros((dh, width), qt_ref.dtype)
    qts = [jnp.concatenate([qt_ref[g, 0] if k == g else zeros for k in groups], axis=0) for g in groups]
    vrows = vst_ref.shape[1]
    lane = lax.broadcasted_iota(I32, (1, width), 1)
    t_row = qi * Q_BLOCK + (lane % Q_BLOCK)
    tq = qi * Q_BLOCK + lax.broadcasted_iota(I32, (1, Q_BLOCK), 1)
    all_heads = lambda a: jnp.concatenate([a] * NSA_HPG, axis=1)
    fdot = lambda a, b: jnp.dot(a, b, preferred_element_type=F32)

    pad = SUBLANES
    ch = min(CMP_CHUNK, nc)
    n_ch = (((qi + 1) * Q_BLOCK - CMP_BLOCK) // CMP_STRIDE) // ch + 1

    def cmp_scores(c, ms):
        off = pl.multiple_of(c * ch, ch)
        ci = off + lax.broadcasted_iota(I32, (ch, 1), 0)
        vis = ci * CMP_STRIDE + (CMP_BLOCK - 1) <= t_row
        out = []
        for g in groups:
            s = jnp.where(vis, fdot(kc_ref[pl.ds(off, ch), :], qts[g]), NEG_INF)
            sc_ref[g, pl.ds(off, ch), :] = s
            out.append(jnp.maximum(ms[g], jnp.max(s, axis=0, keepdims=True)))
        return tuple(out)

    ms = lax.fori_loop(0, n_ch, cmp_scores, tuple(jnp.full((1, width), NEG_INF, F32) for _ in groups))
    ms = [jnp.maximum(m, 0.1 * NEG_INF) for m in ms]

    def cmp_probs(c, accs):
        off = pl.multiple_of(c * ch, ch)
        out = []
        for g in groups:
            e = jnp.exp2(sc_ref[g, pl.ds(off, ch), :] - ms[g])
            sc_ref[g, pl.ds(off, ch), :] = e
            out.append(accs[g] + fdot(vct_ref[g, :, pl.ds(off, ch)], e.astype(BF16)))
        return tuple(out)

    accs = lax.fori_loop(0, n_ch, cmp_probs, tuple(jnp.zeros((vrows, width), F32) for _ in groups))
    inv_l = [1.0 / jnp.maximum(a[dh:dh + 1, :], 1e-30) for a in accs]
    o_cmp = [a[0:dh, :] * i for a, i in zip(accs, inv_l)]
    imp_ref[...] = jnp.zeros_like(imp_ref)

    def cmp_importance(c, _):
        off = pl.multiple_of(c * ch, ch)
        for g in groups:
            p = sc_ref[g, pl.ds(off, ch), :] * inv_l[g]
            imp = p[:, 0:Q_BLOCK]
            for h in range(1, NSA_HPG):
                imp = imp + p[:, h * Q_BLOCK:(h + 1) * Q_BLOCK]
            imp_ref[g, pl.ds(pl.multiple_of(pad + off, pad), ch), :] = imp
        return 0

    lax.fori_loop(0, n_ch, cmp_importance, 0)

    span_w = WINDOW + Q_BLOCK
    woff = pl.multiple_of(jnp.maximum(qi * Q_BLOCK - WINDOW, 0), Q_BLOCK)
    dpos = tq - (woff + lax.broadcasted_iota(I32, (span_w, 1), 0))
    ok = (dpos >= 0) & (dpos < WINDOW)
    win_bias = all_heads(jnp.where(ok, 0.0, NEG_INF))
    o_win = []
    for g in groups:
        sa_ref[g] = fdot(ks_ref[0:KV_TILE, :], qts[g])
        sb_ref[g] = fdot(ks_ref[KV_TILE:2 * KV_TILE, :], qts[g])
        s = fdot(kw_ref[pl.ds(woff, span_w), :], qts[g]) + win_bias
        p = jnp.exp2(s - jnp.max(s, axis=0, keepdims=True)).astype(BF16)
        acc = fdot(vwt_ref[g, :, pl.ds(woff, span_w)], p)
        o_win.append(acc[0:dh, :] / jnp.maximum(acc[dh:dh + 1, :], 1e-30))

    ratio = SLC_BLOCK // CMP_STRIDE
    span = ratio + CMP_BLOCK // CMP_STRIDE - 1
    lead = CMP_BLOCK // CMP_STRIDE - 1
    jb = lax.broadcasted_iota(I32, (ns, 1), 0)
    cur = tq // SLC_BLOCK
    bonus = ((jb == 0) | (jb == cur) | (jb == cur - 1)).astype(F32) * FORCE_BONUS
    jf = jb.astype(F32)
    for g in groups:
        imp_slc = imp_ref[g, pl.ds(pad - lead, ns, stride=ratio), :]
        for sft in range(1, span):
            imp_slc = imp_slc + imp_ref[g, pl.ds(pad - lead + sft, ns, stride=ratio), :]
        score = jnp.where(jb * SLC_BLOCK <= tq, imp_slc + bonus, NEG_INF)
        for _ in range(min(SLC_TOPK, ns)):
            mx = jnp.max(score, axis=0, keepdims=True)
            first = jnp.min(jnp.where(score == mx, jf, float(ns)), axis=0, keepdims=True)
            score = jnp.where(jf == first, -jnp.inf, score)
        selb_ref[g] = jnp.where(score == -jnp.inf, 0.0, NEG_INF)

    blocks = KV_TILE // SLC_BLOCK

    last_tile = ks_ref.shape[0] // KV_TILE - 1

    def tile_bias(g, kt):
        bias_t = selb_ref[g, pl.ds(pl.multiple_of(kt * blocks, blocks), blocks), :]
        bias = jnp.concatenate(
            [jnp.broadcast_to(bias_t[b:b + 1, :], (SLC_BLOCK, Q_BLOCK)) for b in range(blocks)], axis=0)
        return all_heads(bias)

    def scores(kt, dst_ref):
        kt = jnp.minimum(kt, last_tile)
        off = pl.multiple_of(kt * KV_TILE, KV_TILE)
        for g in groups:
            dst_ref[g] = fdot(ks_ref[pl.ds(off, KV_TILE), :], qts[g]) + tile_bias(g, kt)

    def consume(kt, src_ref, carry, masked=False):
        off = pl.multiple_of(kt * KV_TILE, KV_TILE)
        if masked:
            kpos = off + lax.broadcasted_iota(I32, (KV_TILE, 1), 0)
            causal = all_heads(jnp.where(kpos <= tq, 0.0, NEG_INF))
        out = []
        for g in groups:
            m, acc = carry[g]
            s = src_ref[g] + causal if masked else src_ref[g]
            m_new = jnp.maximum(m, jnp.max(s, axis=0, keepdims=True))
            p = jnp.exp2(s - m_new).astype(BF16)
            out.append((m_new, jnp.exp2(m - m_new) * acc + fdot(vst_ref[g, :, pl.ds(off, KV_TILE)], p)))
        return tuple(out)

    front, back = (sa_ref, sb_ref), (sc2_ref, sd_ref)
    for g in groups:
        sa_ref[g] = sa_ref[g] + tile_bias(g, 0)
        sb_ref[g] = sb_ref[g] + tile_bias(g, 1)

    def pair(k, cur, nxt, carry):
        scores(k + 2, nxt[0])
        carry = consume(k, cur[0], carry)
        scores(k + 3, nxt[1])
        return consume(k + 1, cur[1], carry)

    n_before = (qi * Q_BLOCK) // KV_TILE

    def last_pair(k, cur, carry):
        carry = consume(k, cur[0], carry, True)
        return lax.cond(n_before % 2 == 1, lambda c: consume(k + 1, cur[1], c, True), lambda c: c, carry)

    init = tuple((jnp.full((1, width), NEG_INF, F32), jnp.zeros((vrows, width), F32)) for _ in groups)
    n_pairs = n_before // 2
    carry = lax.fori_loop(0, n_pairs // 2,
                          lambda i, c: pair(4 * i + 2, back, front, pair(4 * i, front, back, c)), init)
    k0 = 4 * (n_pairs // 2)
    carry = lax.cond(n_pairs % 2 == 1,
                     lambda c: last_pair(k0 + 2, back, pair(k0, front, back, c)),
                     lambda c: last_pair(k0, front, c), carry)

    for g in groups:
        acc = carry[g][1]
        o_slc = acc[0:dh, :] / jnp.maximum(acc[dh:dh + 1, :], 1e-30)
        gate = jax.nn.sigmoid(g_ref[0, g])
        for h in range(NSA_HPG):
            hs = slice(h * Q_BLOCK, (h + 1) * Q_BLOCK)
            y = (gate[3 * h:3 * h + 1, :] * o_cmp[g][:, hs] + gate[3 * h + 1:3 * h + 2, :] * o_slc[:, hs]
                 + gate[3 * h + 2:3 * h + 3, :] * o_win[g][:, hs])
            row = (g * NSA_HPG + h) * dh
            out_ref[row:row + dh, :] = y.astype(out_ref.dtype)


def _nsa_pair(qt, kc, vct, ks, vst, kw, vwt, gates):
    g, nqt, dh, width = qt.shape
    seq = ks.shape[0]
    nc = kc.shape[0]
    ns = seq // SLC_BLOCK
    resident = lambda a: pl.BlockSpec(a.shape, lambda qi: (0,) * a.ndim, pipeline_mode=pl.Buffered(1))
    return pl.pallas_call(
        functools.partial(_nsa_pair_kernel, nc=nc, ns=ns),
        out_shape=jax.ShapeDtypeStruct((g * NSA_HPG * dh, seq), BF16),
        grid=(nqt,),
        in_specs=[pl.BlockSpec((g, 1, dh, width), lambda qi: (0, qi, 0, 0)),
                  resident(kc), resident(vct), resident(ks), resident(vst), resident(kw), resident(vwt),
                  pl.BlockSpec((1,) + gates.shape[1:], lambda qi: (qi, 0, 0, 0))],
        out_specs=pl.BlockSpec((g * NSA_HPG * dh, Q_BLOCK), lambda qi: (0, qi)),
        scratch_shapes=[pltpu.VMEM((g, nc, width), F32),
                        pltpu.VMEM((g, nc + 2 * SUBLANES, Q_BLOCK), F32),
                        pltpu.VMEM((g, ns, Q_BLOCK), F32),
                        ] + [pltpu.VMEM((g, KV_TILE, width), F32)] * 4,
        compiler_params=_cparams(("arbitrary",)),
        name="sparse_attention",
    )(qt, kc, vct, ks, vst, kw, vwt, gates)


def _split_bf16(x):
    hi = x.astype(BF16)
    return hi, (x - hi.astype(F32)).astype(BF16)


def _gla_kernel(h_ref, wa_ref, ba_ref, ng_ref, y_ref, st_ref):
    @pl.when(pl.program_id(0) == 0)
    def _():
        st_ref[...] = jnp.zeros_like(st_ref)

    kw = GLA_HEADS * GLA_DK
    vw = GLA_HEADS * GLA_DV
    c = GLA_CHUNK
    sb = min(GLA_SUPER, h_ref.shape[0])
    ri = lax.broadcasted_iota(I32, (sb, sb), 0)
    cj = lax.broadcasted_iota(I32, (sb, sb), 1)
    same_chunk = (ri // c) == (cj // c)
    tril = same_chunk & (ri >= cj)
    tril_m = jnp.where(tril, 1.0, 0.0).astype(BF16)
    chunk_m = jnp.where(same_chunk, 1.0, 0.0).astype(BF16)
    ones_c = jnp.ones((c, LANES), BF16)
    krow = lax.broadcasted_iota(I32, (kw, vw), 0) // GLA_DK
    vcol = lax.broadcasted_iota(I32, (kw, vw), 1) // GLA_DV
    blockdiag = krow == vcol
    klane = lax.broadcasted_iota(I32, (1, kw), 1) // GLA_DK
    vlane = lax.broadcasted_iota(I32, (1, vw), 1) // GLA_DV
    va = lax.broadcasted_iota(I32, (vw, vw), 0) // GLA_DV
    vb = lax.broadcasted_iota(I32, (vw, vw), 1) // GLA_DV
    head_avg = jnp.where(va == vb, 1.0 / GLA_DV, 0.0).astype(BF16)
    tn = (((0,), (0,)), ((), ()))

    state = st_ref[...]
    for blk in range(h_ref.shape[0] // sb):
        rows = slice(blk * sb, (blk + 1) * sb)
        q = h_ref[rows, 0:kw] * (GLA_DK ** -0.5)
        k = h_ref[rows, kw:2 * kw]
        v = h_ref[rows, 2 * kw:2 * kw + vw].astype(BF16)
        r = h_ref[rows, 2 * kw + vw:2 * kw + 2 * vw]
        a_low = h_ref[rows, 2 * kw + 2 * vw:2 * kw + 2 * vw + LANES]
        la = jax.nn.log_sigmoid(_hdot(a_low, wa_ref[...]) + ba_ref[...]) / GLA_TAU
        la_hi, la_lo = _split_bf16(la)
        sum_la = lambda mat: (jnp.dot(mat, la_hi, preferred_element_type=F32)
                              + jnp.dot(mat, la_lo, preferred_element_type=F32))
        bcum = sum_la(tril_m)
        b_last = sum_la(chunk_m)
        q_t = (q * jnp.exp(bcum)).astype(BF16)
        k_t = (k * jnp.exp(-bcum)).astype(BF16)
        k_d = (k * jnp.exp(b_last - bcum)).astype(BF16)
        o = jnp.zeros((sb, vw), F32)
        for hd in range(GLA_HEADS):
            attn = _bdot_nt(jnp.where(klane == hd, q_t, jnp.zeros_like(q_t)), k_t)
            attn = jnp.where(tril, attn, 0.0).astype(BF16)
            o = o + jnp.where(vlane == hd, jnp.dot(attn, v, preferred_element_type=F32), 0.0)
        inter = []
        for n in range(sb // c):
            cs = slice(n * c, (n + 1) * c)
            inter.append(jnp.dot(q_t[cs], state.astype(BF16), preferred_element_type=F32))
            kv = jnp.where(blockdiag, lax.dot_general(k_d[cs], v[cs], tn, preferred_element_type=F32), 0.0)
            tot = (lax.dot_general(la_hi[cs], ones_c, tn, preferred_element_type=F32)
                   + lax.dot_general(la_lo[cs], ones_c, tn, preferred_element_type=F32))
            decay = jnp.exp(tot)
            state = jnp.concatenate([decay] * (vw // LANES), axis=1) * state + kv
        o = o + jnp.concatenate(inter, axis=0)
        sq_hi, sq_lo = _split_bf16(o * o)
        ms = (jnp.dot(sq_hi, head_avg, preferred_element_type=F32)
              + jnp.dot(sq_lo, head_avg, preferred_element_type=F32))
        o = o * lax.rsqrt(ms + EPS)
        y_ref[rows, :] = (o * ng_ref[...] * (r * jax.nn.sigmoid(r))).astype(y_ref.dtype)
    st_ref[...] = state


def _gla_mixer(hd, w_a2p, b_a, norm_g, l):
    seq, wid = hd.shape
    tb = min(seq, 512)
    kw = GLA_HEADS * GLA_DK
    vw = GLA_HEADS * GLA_DV
    return pl.pallas_call(
        _gla_kernel,
        out_shape=jax.ShapeDtypeStruct((seq, vw), BF16),
        grid=(seq // tb,),
        in_specs=[pl.BlockSpec((tb, wid), lambda i: (i, 0)), _of_layer(w_a2p, l), _of_layer(b_a, l),
                  _of_layer(norm_g, l)],
        out_specs=pl.BlockSpec((tb, vw), lambda i: (i, 0)),
        scratch_shapes=[pltpu.VMEM((kw, vw), F32)],
        compiler_params=_cparams(("arbitrary",)),
        name="gla_mixer",
    )(hd, w_a2p, b_a, norm_g)


def _merge_kernel(x_ref, g_ref, ys_ref, yn_ref, yg_ref, wgm_ref, ws_ref, wn_ref, wg_ref, wo_ref, o_ref):
    x = x_ref[...]
    d = x.shape[1]
    xn = _rms(x, g_ref[...]).astype(BF16)
    gm = jax.nn.sigmoid(jnp.dot(xn, wgm_ref[...], preferred_element_type=F32))
    mixed = (gm[:, 0:d] * jnp.dot(ys_ref[...], ws_ref[...], preferred_element_type=F32)
             + gm[:, d:2 * d] * lax.dot_general(yn_ref[...], wn_ref[...], (((0,), (0,)), ((), ())),
                                                preferred_element_type=F32)
             + gm[:, 2 * d:3 * d] * jnp.dot(yg_ref[...], wg_ref[...], preferred_element_type=F32))
    o_ref[...] = x + _bdot(mixed, wo_ref[...])


def _merge(x, g, ys, yn, yg, wgm, ws, wn, wg, wo, l):
    seq, d = x.shape
    tm = min(seq, 1024)
    row = lambda a: pl.BlockSpec((tm, a.shape[1]), lambda i: (i, 0))
    return pl.pallas_call(
        _merge_kernel,
        out_shape=jax.ShapeDtypeStruct((seq, d), F32),
        grid=(seq // tm,),
        in_specs=[row(x), _of_layer(g, l), row(ys), pl.BlockSpec((yn.shape[0], tm), lambda i: (0, i)), row(yg)]
                 + [_of_layer(a, l) for a in (wgm, ws, wn, wg, wo)],
        out_specs=row(x),
        compiler_params=_cparams(("parallel",)),
        name="merge",
    )(x, g, ys, yn, yg, wgm, ws, wn, wg, wo)


def _ffn_kernel(x_ref, xp_ref, g_ref, wug_ref, wuv_ref, cwg_ref, cwv_ref, cbg_ref, cbv_ref, wd_ref,
                fg_ref, o_ref, xn_ref, *, final):
    i = pl.program_id(0)
    j = pl.program_id(1)
    tm = x_ref.shape[0]
    halo = xp_ref.shape[0]

    @pl.when(j == 0)
    def _():
        xn_ref[0:halo, :] = jnp.where(i == 0, 0.0, _rms(xp_ref[...], g_ref[...])).astype(BF16)
        xn_ref[halo:halo + tm, :] = _rms(x_ref[...], g_ref[...]).astype(BF16)

    xn = xn_ref[...]

    def conv(w_ref, cw_ref, cb_ref):
        h = jnp.dot(xn, w_ref[...], preferred_element_type=F32)
        hc = cb_ref[...]
        for t in range(CONV_WIDTH):
            sh = CONV_WIDTH - 1 - t
            hs = h if sh == 0 else pltpu.roll(h, sh, 0)
            hc = hc + cw_ref[t:t + 1, :] * hs[halo:halo + tm, :]
        return hc

    act = jax.nn.gelu(conv(wug_ref, cwg_ref, cbg_ref)) * conv(wuv_ref, cwv_ref, cbv_ref)
    part = _bdot(act, wd_ref[...])

    @pl.when(j == 0)
    def _():
        o_ref[...] = x_ref[...] + part

    @pl.when(j > 0)
    def _():
        o_ref[...] = o_ref[...] + part

    if final:
        @pl.when(j == pl.num_programs(1) - 1)
        def _():
            o_ref[...] = _rms(o_ref[...], fg_ref[...])


def _ffn(x, g, w_up, conv_w, conv_b, w_down, final_g, final, l):
    seq, d = x.shape
    dff = w_down.shape[1]
    tm = min(seq, 1024)
    nj = 2
    tn = dff // nj
    halo = SUBLANES
    hb = tm // halo
    return pl.pallas_call(
        functools.partial(_ffn_kernel, final=final),
        out_shape=jax.ShapeDtypeStruct((seq, d), F32),
        grid=(seq // tm, nj),
        in_specs=[pl.BlockSpec((tm, d), lambda i, j: (i, 0)),
                  pl.BlockSpec((halo, d), lambda i, j: (jnp.maximum(i * hb - 1, 0), 0)),
                  _of_layer(g, l),
                  pl.BlockSpec((None, d, tn), lambda i, j: (l, 0, j)),
                  pl.BlockSpec((None, d, tn), lambda i, j: (l, 0, nj + j)),
                  pl.BlockSpec((None, CONV_WIDTH, tn), lambda i, j: (l, 0, j)),
                  pl.BlockSpec((None, CONV_WIDTH, tn), lambda i, j: (l, 0, nj + j)),
                  pl.BlockSpec((None, 1, tn), lambda i, j: (l, 0, j)),
                  pl.BlockSpec((None, 1, tn), lambda i, j: (l, 0, nj + j)),
                  pl.BlockSpec((None, tn, d), lambda i, j: (l, j, 0)),
                  pl.BlockSpec((1, d), lambda i, j: (0, 0))],
        out_specs=pl.BlockSpec((tm, d), lambda i, j: (i, 0)),
        scratch_shapes=[pltpu.VMEM((halo + tm, d), BF16)],
        compiler_params=_cparams(("parallel", "arbitrary")),
        name="conv_ffn",
    )(x, x, g, w_up, w_up, conv_w, conv_w, conv_b, conv_b, w_down, final_g)


def _split_w_in(w):
    depth, d = w.shape[:2]
    w = w.astype(BF16)
    s5w = d // 4
    nsa_w = NSA_HEADS * HEAD_DIM
    kvw = NSA_KV_GROUPS * HEAD_DIM
    kw = GLA_HEADS * GLA_DK
    vw = GLA_HEADS * GLA_DV
    sizes = (s5w, nsa_w, kvw, kvw, kvw, kvw, kvw, kvw, 3 * NSA_HEADS, kw, kw, vw, 16, vw, 3 * d)
    assert sum(sizes) == w.shape[2]
    offs = [0]
    for sz in sizes:
        offs.append(offs[-1] + sz)
    col = lambda n: w[:, :, offs[n]:offs[n + 1]]
    (u, q, kc, vc, ks, vs, kwn, vwn, gn, gq, gk, gv, ga, gr, gm) = [col(n) for n in range(len(sizes))]
    padl = lambda a: jnp.pad(a, ((0, 0), (0, 0), (0, LANES - a.shape[2])))
    tr = lambda a: jnp.swapaxes(a, 1, 2)
    per_group = 3 * NSA_HPG
    gates_t = jnp.pad(tr(gn).reshape(depth, NSA_KV_GROUPS, per_group, d),
                      ((0, 0), (0, 0), (0, BF16_SUBLANES - per_group), (0, 0))).reshape(depth, -1, d)
    return dict(u=u, k=jnp.concatenate([kc, ks, kwn], axis=2), vc=vc,
                d=jnp.concatenate([gq, gk, gv, gr, padl(ga)], axis=2),
                qt=tr(q), vt=tr(jnp.concatenate([vs, vwn], axis=2)), gt=gates_t, gm=gm)


def kernel(x, positions, norm1_g, w_in, s5_lam_re, s5_lam_im, s5_log_dt, s5_b_re, s5_b_im, s5_c_re, s5_c_im, s5_d, s5_w_glu, s5_b_glu, nsa_pe_k, nsa_pe_v, nsa_ck_w1, nsa_ck_w2, nsa_cv_w1, nsa_cv_w2, gla_w_a2, gla_b_a, gla_norm_g, w_br_s5, w_br_nsa, w_br_gla, w_out, norm2_g, ffn_w_up, ffn_conv_w, ffn_conv_b, ffn_w_down, final_g):
    bsz, seq, d = x.shape
    depth = w_in.shape[0]
    assert bsz == 1 and seq % KV_TILE == 0 and seq >= 2 * KV_TILE and d % LANES == 0
    g = NSA_KV_GROUPS
    dh = HEAD_DIM
    rows = lambda a: a.reshape(depth, 1, -1)
    bf = lambda a: a.astype(BF16)

    w = _split_w_in(w_in)
    g1 = rows(norm1_g)
    pe = jnp.stack([nsa_pe_k.reshape(depth, 1, -1), nsa_pe_v.reshape(depth, 1, -1)], axis=1)
    pe = jnp.broadcast_to(pe, (depth, 2, SUBLANES, pe.shape[-1]))
    w1 = bf(jnp.stack([nsa_ck_w1, nsa_cv_w1], axis=1))
    taps = w1.reshape(depth, 2, 1, CMP_BLOCK, dh, w1.shape[-1])
    w1g = jnp.concatenate(
        [jnp.pad(taps, ((0, 0),) * 4 + ((gi * dh, (g - 1 - gi) * dh), (0, 0))) for gi in range(g)], axis=2)
    w2k = bf(jnp.stack([jnp.pad(nsa_ck_w2, ((0, 0), (0, 0), (gi * dh, (g - 1 - gi) * dh))) for gi in range(g)],
                       axis=1))
    w2v = bf(jnp.swapaxes(nsa_cv_w2, 1, 2))
    w_a2p = jnp.pad(gla_w_a2, ((0, 0), (0, LANES - gla_w_a2.shape[1]), (0, 0)))
    w_glu, wb_s5, wb_nsa, wb_gla, wo = bf(s5_w_glu), bf(w_br_s5), bf(w_br_nsa), bf(w_br_gla), bf(w_out)
    w_up, w_down = bf(ffn_w_up), bf(ffn_w_down)

    xs = x.reshape(seq, d)
    tabs = _rope_tables(positions.reshape(seq))

    for l in range(depth):
        u_s5, qt, x_cmp, ks, kwn, vst, vwt, gates, hd = _project(xs, g1, w, tabs, l)

        prep = _s5_prep(s5_lam_re[l], s5_lam_im[l], s5_log_dt[l], s5_b_re[l], s5_b_im[l],
                        s5_c_re[l], s5_c_im[l])
        y_s5 = _s5_mixer(u_s5, prep, rows(s5_d), w_glu, rows(s5_b_glu), l)

        kc, vct = _compress(x_cmp, w1g, w1, w2k, w2v, pe, l)
        y_nsa = _nsa_pair(qt, kc, vct, ks, vst, kwn, vwt, gates)

        y_gla = _gla_mixer(hd, w_a2p, rows(gla_b_a), rows(gla_norm_g), l)

        xs = _merge(xs, g1, y_s5, y_nsa, y_gla, w["gm"], wb_s5, wb_nsa, wb_gla, wo, l)
        xs = _ffn(xs, rows(norm2_g), w_up, ffn_conv_w, rows(ffn_conv_b), w_down, final_g.reshape(1, -1),
                  final=(l == depth - 1), l=l)
    return xs.reshape(bsz, seq, d)
```
